```python
import jax, jax.numpy as jnp
from jax import lax
import numpy as np

D_MODEL = 1024
BATCH = 4
SEQ = 4096
DEPTH = 4

N_HEADS = D_MODEL // 128
QK_NOPE_DIM = 64
QK_ROPE_DIM = 32
QK_HEAD_DIM = QK_NOPE_DIM + QK_ROPE_DIM
V_HEAD_DIM = 64
ATTN_WIDTH = N_HEADS * V_HEAD_DIM
Q_LORA_RANK = D_MODEL // 4
KV_LORA_RANK = D_MODEL // 4
ROPE_THETA = 10000.0
Q_BLOCK = 128
FOURIER_WIDTH = D_MODEL // 2
FOURIER_GROUP_DIM = 128
N_FOURIER_GROUPS = FOURIER_WIDTH // FOURIER_GROUP_DIM
N_BRANCHES = 2
NORM_EPS = 1e-6
IN_WIDTH = (Q_LORA_RANK + KV_LORA_RANK + QK_ROPE_DIM + ATTN_WIDTH
            + 2 * FOURIER_WIDTH + N_BRANCHES * D_MODEL)

kernel_name = "hybrid_mla_fourier_gated_encoder"


def _rms_norm(x, g):
    xf = x.astype(jnp.float32)
    y = xf * lax.rsqrt(jnp.mean(xf * xf, axis=-1, keepdims=True) + NORM_EPS)
    return (y * g.astype(jnp.float32)).astype(x.dtype)


def _rope_tables(seq_len, dtype):
    half = QK_ROPE_DIM // 2
    inv_freq = ROPE_THETA ** (-jnp.arange(half, dtype=jnp.float32) / half)
    pos = jnp.arange(seq_len, dtype=jnp.float32)
    ang = pos[:, None] * inv_freq[None, :]
    return jnp.cos(ang).astype(dtype), jnp.sin(ang).astype(dtype)


def _apply_rope(x, cos, sin):
    half = QK_ROPE_DIM // 2
    x1, x2 = x[..., :half], x[..., half:]
    c, s = cos[:, None, :], sin[:, None, :]
    return jnp.concatenate([x1 * c - x2 * s, x2 * c + x1 * s], axis=-1)


def _split_in(p):
    sizes = [Q_LORA_RANK, KV_LORA_RANK, QK_ROPE_DIM, ATTN_WIDTH,
             FOURIER_WIDTH, FOURIER_WIDTH, D_MODEL]
    idx = []
    acc = 0
    for s in sizes:
        acc += s
        idx.append(acc)
    return jnp.split(p, idx, axis=-1)


def _bidirectional_attention(q, k, v):
    b, h, s, dk = q.shape
    nb = s // Q_BLOCK
    scale = QK_HEAD_DIM ** -0.5
    qb = q.reshape(b, h, nb, Q_BLOCK, dk).transpose(2, 0, 1, 3, 4)

    def one_block(q_blk):
        sc = jnp.einsum('bhqd,bhkd->bhqk', q_blk, k).astype(jnp.float32) * scale
        p = jax.nn.softmax(sc, axis=-1)
        return jnp.einsum('bhqk,bhkd->bhqd', p.astype(v.dtype), v)

    o = lax.map(one_block, qb)
    return o.transpose(1, 2, 0, 3, 4).reshape(b, h, s, V_HEAD_DIM)


def _mla_branch(c_q, c_kv, k_pe, q_latent_g, kv_latent_g, w_uq, w_ukv,
                q_head_g, k_head_g, cos, sin):
    b, s, _ = c_q.shape
    q = (_rms_norm(c_q, q_latent_g) @ w_uq).reshape(b, s, N_HEADS, QK_HEAD_DIM)
    kv = (_rms_norm(c_kv, kv_latent_g) @ w_ukv).reshape(b, s, N_HEADS, QK_NOPE_DIM + V_HEAD_DIM)
    k_nope, v = kv[..., :QK_NOPE_DIM], kv[..., QK_NOPE_DIM:]
    k_pe_h = jnp.broadcast_to(k_pe[:, :, None, :], (b, s, N_HEADS, QK_ROPE_DIM))
    k = jnp.concatenate([k_nope, k_pe_h], axis=-1)
    q = _rms_norm(q, q_head_g)
    k = _rms_norm(k, k_head_g)
    q = jnp.concatenate([q[..., :QK_NOPE_DIM], _apply_rope(q[..., QK_NOPE_DIM:], cos, sin)], axis=-1)
    k = jnp.concatenate([k[..., :QK_NOPE_DIM], _apply_rope(k[..., QK_NOPE_DIM:], cos, sin)], axis=-1)
    o = _bidirectional_attention(q.transpose(0, 2, 1, 3), k.transpose(0, 2, 1, 3),
                                 v.transpose(0, 2, 1, 3))
    return o.transpose(0, 2, 1, 3).reshape(b, s, ATTN_WIDTH)


def _fourier_branch(u):
    b, s, _ = u.shape
    ug = u.astype(jnp.float32).reshape(b, s, N_FOURIER_GROUPS, FOURIER_GROUP_DIM)
    f = jnp.real(jnp.fft.fft2(ug, axes=(1, 3), norm='ortho'))
    return f.reshape(b, s, FOURIER_WIDTH).astype(u.dtype)


def setup_inputs(seed: int = 0) -> dict:
    key = jax.random.key(seed)
    ks = jax.random.split(key, 14)
    L, D = DEPTH, D_MODEL

    def nrm(k, shape, fan_in):
        return jax.random.normal(k, shape, jnp.float32) * fan_in ** -0.5

    def gain(k, shape):
        return 1.0 + 0.05 * jax.random.normal(k, shape, jnp.float32)

    return {
        "x": jax.random.normal(ks[0], (BATCH, SEQ, D), jnp.float32),
        "norm_g": gain(ks[1], (L, D)),
        "w_in": nrm(ks[2], (L, D, IN_WIDTH), D),
        "q_latent_g": gain(ks[3], (L, Q_LORA_RANK)),
        "kv_latent_g": gain(ks[4], (L, KV_LORA_RANK)),
        "w_uq": nrm(ks[5], (L, Q_LORA_RANK, N_HEADS * QK_HEAD_DIM), Q_LORA_RANK),
        "w_ukv": nrm(ks[6], (L, KV_LORA_RANK, N_HEADS * (QK_NOPE_DIM + V_HEAD_DIM)), KV_LORA_RANK),
        "q_head_g": gain(ks[7], (L, QK_HEAD_DIM)),
        "k_head_g": gain(ks[8], (L, QK_HEAD_DIM)),
        "w_attn_proj": nrm(ks[9], (L, ATTN_WIDTH, D), ATTN_WIDTH),
        "w_fourier_proj": nrm(ks[10], (L, FOURIER_WIDTH, D), FOURIER_WIDTH),
        "b_merge": 0.1 * jax.random.normal(ks[11], (L, N_BRANCHES, D), jnp.float32),
        "w_out": nrm(ks[12], (L, D, D), D) * (2 * DEPTH) ** -0.5,
    }


def reference(x, norm_g, w_in, q_latent_g, kv_latent_g, w_uq, w_ukv, q_head_g,
              k_head_g, w_attn_proj, w_fourier_proj, b_merge, w_out):
    _, s, _ = x.shape
    cos, sin = _rope_tables(s, x.dtype)
    for l in range(DEPTH):
        h = _rms_norm(x, norm_g[l])
        p = h @ w_in[l]
        c_q, c_kv, k_pe, z_a, u_f, z_f, g_a, g_f = _split_in(p)
        o_a = _mla_branch(c_q, c_kv, k_pe, q_latent_g[l], kv_latent_g[l], w_uq[l], w_ukv[l],
                          q_head_g[l], k_head_g[l], cos, sin)
        y_a = (o_a * jax.nn.silu(z_a)) @ w_attn_proj[l]
        y_f = (_fourier_branch(u_f) * jax.nn.silu(z_f)) @ w_fourier_proj[l]
        m = (jax.nn.sigmoid(g_a + b_merge[l, 0]) * y_a
             + jax.nn.sigmoid(g_f + b_merge[l, 1]) * y_f)
        x = x + m @ w_out[l]
    return x
```

```python
import functools
import math

import numpy as np
import jax
import jax.numpy as jnp
from jax import lax
from jax.experimental import pallas as pl
from jax.experimental.pallas import tpu as pltpu

D_MODEL = 1024
BATCH = 4
SEQ = 4096
DEPTH = 4
N_HEADS = 8
QK_NOPE = 64
QK_ROPE = 32
QK_DIM = QK_NOPE + QK_ROPE
V_DIM = 64
HEAD_PAD = 128
ATTN_WIDTH = N_HEADS * V_DIM
Q_RANK = 256
KV_RANK = 256
ROPE_THETA = 10000.0
FOURIER_WIDTH = 512
GROUP_DIM = 128
N_GROUPS = 4
NORM_EPS = 1e-6
RADIX = 64
LAT_WIDTH = 640

F32 = jnp.float32
BF16 = jnp.bfloat16

VMEM_LIMIT = 48 * 1024 * 1024

TOKENS = BATCH * SEQ
PROJ_TM = 512
ATTN_TQ = 256
DFT_A_TN = 8192
DFT_B_KB = 8
MERGE_TM = 512

Q_SCALE = (QK_DIM ** -0.5) * math.log2(math.e)


def _dot(a, b):
    return jnp.dot(a, b, preferred_element_type=F32)


def _rms(x, axis):
    return lax.rsqrt(jnp.mean(x * x, axis=axis, keepdims=True) + NORM_EPS)


def _proj_body(x_ref, ng_ref, w1_ref, qlg_ref, kvlg_ref, wq_ref, wqp_ref, wknT_ref, wv_ref,
               vones_ref, gq_ref, gqp_ref, gk_ref, gkp_ref, cosq_ref, sinq_ref, cost_ref,
               sint_ref, fc_ref, q_out, kt_out, v_out, wr_out, wi_out):
    x = x_ref[...]
    hb = (x * _rms(x, -1) * ng_ref[...]).astype(BF16)
    p = _dot(hb, w1_ref[...])
    cq = p[:, 0:Q_RANK]
    ckv = p[:, Q_RANK:Q_RANK + KV_RANK]
    kpe = p[:, 512:LAT_WIDTH]
    u = p[:, LAT_WIDTH:LAT_WIDTH + FOURIER_WIDTH]

    ch_scale = GROUP_DIM ** -0.5
    fc = fc_ref[...].astype(BF16)
    for g in range(N_GROUPS):
        ug = u[:, g * GROUP_DIM:(g + 1) * GROUP_DIM].astype(BF16)
        wg = _dot(ug, fc) * ch_scale
        wr_out[:, g * GROUP_DIM:(g + 1) * GROUP_DIM] = wg[:, :GROUP_DIM].astype(BF16)
        wi_out[:, g * GROUP_DIM:(g + 1) * GROUP_DIM] = wg[:, GROUP_DIM:].astype(BF16)

    cqn = (cq * _rms(cq, -1) * qlg_ref[...]).astype(BF16)
    ckvn_f = ckv * _rms(ckv, -1) * kvlg_ref[...]
    ckvn = ckvn_f.astype(BF16)

    v_out[...] = (_dot(ckvn, wv_ref[...]) + vones_ref[...]).astype(BF16)

    q_raw = _dot(cqn, wq_ref[...])
    q_par = _dot(cqn, wqp_ref[...])
    gcq = gq_ref[...] * cosq_ref[...]
    gsq = gqp_ref[...] * sinq_ref[...]
    for h in range(N_HEADS):
        sl = slice(h * HEAD_PAD, (h + 1) * HEAD_PAD)
        qh = q_raw[:, sl]
        r = lax.rsqrt(jnp.sum(qh * qh, axis=-1, keepdims=True) * (1.0 / QK_DIM) + NORM_EPS)
        q_out[:, sl] = ((qh * gcq + q_par[:, sl] * gsq) * (r * Q_SCALE)).astype(BF16)

    ckvn_t = ckvn_f.T.astype(BF16)
    kn_t = _dot(wknT_ref[...], ckvn_t)
    kp = kpe.T[0:QK_ROPE, :]
    half = QK_ROPE // 2
    kp_sw = jnp.concatenate([kp[half:], kp[:half]], axis=0)
    gk = gk_ref[...]
    rope = kp * (gk[QK_NOPE:] * cost_ref[...]) + kp_sw * (gkp_ref[...] * sint_ref[...])
    ss_pe = jnp.sum(kp * kp, axis=0, keepdims=True)
    zeros = jnp.zeros((HEAD_PAD - QK_DIM, kp.shape[1]), BF16)
    for h in range(N_HEADS):
        kn = kn_t[h * QK_NOPE:(h + 1) * QK_NOPE, :]
        ss = jnp.sum(kn * kn, axis=0, keepdims=True) + ss_pe
        r = lax.rsqrt(ss * (1.0 / QK_DIM) + NORM_EPS)
        base = h * HEAD_PAD
        kt_out[0, base:base + QK_NOPE, :] = (kn * gk[:QK_NOPE] * r).astype(BF16)
        kt_out[0, base + QK_NOPE:base + QK_DIM, :] = (rope * r).astype(BF16)
        kt_out[0, base + QK_DIM:base + HEAD_PAD, :] = zeros


def _proj_call(x2, ng, w1, qlg, kvlg, wq, wqp, wknT, wv, vones, gq, gqp, gk, gkp,
               cosq, sinq, cost, sint, fc):
    tm = PROJ_TM
    spb = SEQ // tm
    const = lambda shape: pl.BlockSpec(shape, lambda i: (0,) * len(shape))
    in_specs = [
        pl.BlockSpec((tm, D_MODEL), lambda i: (i, 0)),
        const((1, D_MODEL)),
        const((D_MODEL, LAT_WIDTH + FOURIER_WIDTH)),
        const((1, Q_RANK)),
        const((1, KV_RANK)),
        const((Q_RANK, N_HEADS * HEAD_PAD)),
        const((Q_RANK, N_HEADS * HEAD_PAD)),
        const((N_HEADS * QK_NOPE, KV_RANK)),
        const((KV_RANK, N_HEADS * HEAD_PAD)),
        const((1, N_HEADS * HEAD_PAD)),
        const((1, HEAD_PAD)),
        const((1, HEAD_PAD)),
        const((QK_DIM, 1)),
        const((QK_ROPE, 1)),
        pl.BlockSpec((tm, HEAD_PAD), lambda i: (i % spb, 0)),
        pl.BlockSpec((tm, HEAD_PAD), lambda i: (i % spb, 0)),
        pl.BlockSpec((QK_ROPE, tm), lambda i: (0, i % spb)),
        pl.BlockSpec((QK_ROPE, tm), lambda i: (0, i % spb)),
        const((GROUP_DIM, 2 * GROUP_DIM)),
    ]
    out_shape = [
        jax.ShapeDtypeStruct((TOKENS, N_HEADS * HEAD_PAD), BF16),
        jax.ShapeDtypeStruct((BATCH, N_HEADS * HEAD_PAD, SEQ), BF16),
        jax.ShapeDtypeStruct((TOKENS, N_HEADS * HEAD_PAD), BF16),
        jax.ShapeDtypeStruct((TOKENS, FOURIER_WIDTH), BF16),
        jax.ShapeDtypeStruct((TOKENS, FOURIER_WIDTH), BF16),
    ]
    out_specs = [
        pl.BlockSpec((tm, N_HEADS * HEAD_PAD), lambda i: (i, 0)),
        pl.BlockSpec((1, N_HEADS * HEAD_PAD, tm), lambda i: (i // spb, 0, i % spb)),
        pl.BlockSpec((tm, N_HEADS * HEAD_PAD), lambda i: (i, 0)),
        pl.BlockSpec((tm, FOURIER_WIDTH), lambda i: (i, 0)),
        pl.BlockSpec((tm, FOURIER_WIDTH), lambda i: (i, 0)),
    ]
    return pl.pallas_call(
        _proj_body,
        grid=(TOKENS // tm,),
        in_specs=in_specs,
        out_specs=out_specs,
        out_shape=out_shape,
        compiler_params=pltpu.CompilerParams(
            dimension_semantics=("arbitrary",), vmem_limit_bytes=VMEM_LIMIT),
        name="proj",
    )(x2, ng, w1, qlg, kvlg, wq, wqp, wknT, wv, vones, gq, gqp, gk, gkp,
      cosq, sinq, cost, sint, fc)


def _attn_body(q_ref, kt_ref, v_ref, o_ref):
    outs = []
    for hh in range(2):
        sl = slice(hh * HEAD_PAD, (hh + 1) * HEAD_PAD)
        s = _dot(q_ref[0, :, sl], kt_ref[0, sl, :])
        m = jnp.max(s, axis=-1, keepdims=True)
        p = jnp.exp2(s - m).astype(BF16)
        outs.append(_dot(p, v_ref[0, :, sl]))
    even, odd = outs
    lane = lax.broadcasted_iota(jnp.int32, even.shape, 1)
    num = jnp.where(lane < V_DIM, even, odd)
    den = jnp.where(lane < V_DIM, pltpu.roll(even, V_DIM, 1), pltpu.roll(odd, V_DIM, 1))
    o_ref[0] = (num / den).astype(BF16)


def _attn_call(q3, kt, v3):
    tq = ATTN_TQ
    return pl.pallas_call(
        _attn_body,
        grid=(BATCH, N_HEADS // 2, SEQ // tq),
        in_specs=[
            pl.BlockSpec((1, tq, 2 * HEAD_PAD), lambda b, j, i: (b, i, j)),
            pl.BlockSpec((1, 2 * HEAD_PAD, SEQ), lambda b, j, i: (b, j, 0)),
            pl.BlockSpec((1, SEQ, 2 * HEAD_PAD), lambda b, j, i: (b, 0, j)),
        ],
        out_specs=pl.BlockSpec((1, tq, 2 * V_DIM), lambda b, j, i: (b, i, j)),
        out_shape=jax.ShapeDtypeStruct((BATCH, SEQ, ATTN_WIDTH), BF16),
        compiler_params=pltpu.CompilerParams(
            dimension_semantics=("arbitrary", "arbitrary", "arbitrary"),
            vmem_limit_bytes=VMEM_LIMIT),
        name="attn",
    )(q3, kt, v3)


def _dft_a_body(wr_ref, wi_ref, cs_ref, xr_out, xi_out):
    cs = cs_ref[...].astype(BF16)
    a = _dot(cs, wr_ref[0])
    b = _dot(cs, wi_ref[0])
    xr_out[0] = (a[:RADIX] + b[RADIX:]).astype(BF16)
    xi_out[0] = (b[:RADIX] - a[RADIX:]).astype(BF16)


def _dft_a_call(wr, wi, cs):
    tn = DFT_A_TN
    width = RADIX * FOURIER_WIDTH
    spec = pl.BlockSpec((1, RADIX, tn), lambda b, j: (b, 0, j))
    return pl.pallas_call(
        _dft_a_body,
        grid=(BATCH, width // tn),
        in_specs=[spec, spec, pl.BlockSpec((2 * RADIX, RADIX), lambda b, j: (0, 0))],
        out_specs=[spec, spec],
        out_shape=[jax.ShapeDtypeStruct((BATCH, RADIX, width), BF16)] * 2,
        compiler_params=pltpu.CompilerParams(
            dimension_semantics=("arbitrary", "arbitrary"), vmem_limit_bytes=VMEM_LIMIT),
        name="dft_a",
    )(wr, wi, cs)


def _dft_b_body(xr_ref, xi_ref, bc_ref, bs_ref, f_out):
    for t in range(DFT_B_KB):
        y = (_dot(bc_ref[t].astype(BF16), xr_ref[0, t])
             + _dot(bs_ref[t].astype(BF16), xi_ref[0, t]))
        f_out[0, :, t * FOURIER_WIDTH:(t + 1) * FOURIER_WIDTH] = y.astype(BF16)


def _dft_b_call(xr, xi, bc, bs):
    kb = DFT_B_KB
    xspec = pl.BlockSpec((1, kb, RADIX, FOURIER_WIDTH), lambda b, j: (b, j, 0, 0))
    tspec = pl.BlockSpec((kb, RADIX, RADIX), lambda b, j: (j, 0, 0))
    return pl.pallas_call(
        _dft_b_body,
        grid=(BATCH, RADIX // kb),
        in_specs=[xspec, xspec, tspec, tspec],
        out_specs=pl.BlockSpec((1, RADIX, kb * FOURIER_WIDTH), lambda b, j: (b, 0, j)),
        out_shape=jax.ShapeDtypeStruct((BATCH, RADIX, RADIX * FOURIER_WIDTH), BF16),
        compiler_params=pltpu.CompilerParams(
            dimension_semantics=("arbitrary", "arbitrary"), vmem_limit_bytes=VMEM_LIMIT),
        name="dft_b",
    )(xr, xi, bc, bs)


def _merge_body(x_ref, oa_ref, f_ref, ng_ref, wg_ref, wa_ref, wf_ref, bm_ref, wo_ref, out_ref):
    x = x_ref[...]
    hb = (x * _rms(x, -1) * ng_ref[...]).astype(BF16)
    gates = _dot(hb, wg_ref[...])
    z_a = gates[:, 0:ATTN_WIDTH]
    z_f = gates[:, ATTN_WIDTH:ATTN_WIDTH + FOURIER_WIDTH]
    g_a = gates[:, 1024:1024 + D_MODEL]
    g_f = gates[:, 1024 + D_MODEL:]
    ya = _dot((oa_ref[...].astype(F32) * (z_a * jax.nn.sigmoid(z_a))).astype(BF16), wa_ref[...])
    yf = _dot((f_ref[...].astype(F32) * (z_f * jax.nn.sigmoid(z_f))).astype(BF16), wf_ref[...])
    bm = bm_ref[...]
    m = jax.nn.sigmoid(g_a + bm[0:1]) * ya + jax.nn.sigmoid(g_f + bm[1:2]) * yf
    out_ref[...] = x + _dot(m.astype(BF16), wo_ref[...])


def _merge_call(x2, oa, f, ng, wg, wa, wf, bm, wo):
    tm = MERGE_TM
    const = lambda shape: pl.BlockSpec(shape, lambda i: (0,) * len(shape))
    return pl.pallas_call(
        _merge_body,
        grid=(TOKENS // tm,),
        in_specs=[
            pl.BlockSpec((tm, D_MODEL), lambda i: (i, 0)),
            pl.BlockSpec((tm, ATTN_WIDTH), lambda i: (i, 0)),
            pl.BlockSpec((tm, FOURIER_WIDTH), lambda i: (i, 0)),
            const((1, D_MODEL)),
            const((D_MODEL, 3 * D_MODEL)),
            const((ATTN_WIDTH, D_MODEL)),
            const((FOURIER_WIDTH, D_MODEL)),
            const((2, D_MODEL)),
            const((D_MODEL, D_MODEL)),
        ],
        out_specs=pl.BlockSpec((tm, D_MODEL), lambda i: (i, 0)),
        out_shape=jax.ShapeDtypeStruct((TOKENS, D_MODEL), F32),
        compiler_params=pltpu.CompilerParams(
            dimension_semantics=("arbitrary",), vmem_limit_bytes=VMEM_LIMIT),
        name="merge",
    )(x2, oa, f, ng, wg, wa, wf, bm, wo)


def _dft_tables():
    def cs(num, den):
        ang = 2.0 * np.pi * (num % den).astype(np.float64) / den
        return np.cos(ang), np.sin(ang)

    c = np.arange(GROUP_DIM)
    cc, sc = cs(np.outer(c, c), GROUP_DIM)
    fc = np.concatenate([cc, -sc], axis=1)

    k = np.arange(RADIX)
    ca, sa = cs(np.outer(k, k), RADIX)
    stage_a = np.concatenate([ca, sa], axis=0) / 8.0

    k2 = k[:, None, None]
    k1 = k[None, :, None]
    s1 = k[None, None, :]
    cb, sb = cs(s1 * (RADIX * k1 + k2), SEQ)
    return (jnp.asarray(fc, F32), jnp.asarray(stage_a, F32),
            jnp.asarray(cb / 8.0, F32), jnp.asarray(sb / 8.0, F32))


def _head_column_maps():
    lanes = np.arange(HEAD_PAD)
    own = np.where(lanes < QK_DIM, lanes, 0)
    own_mask = (lanes < QK_DIM).astype(np.float32)
    half = QK_ROPE // 2
    partner = np.where(lanes < QK_NOPE + half, lanes + half, lanes - half)
    par_mask = ((lanes >= QK_NOPE) & (lanes < QK_DIM)).astype(np.float32)
    partner = np.where(par_mask > 0, partner, 0)
    return own, own_mask, partner, par_mask


def _rope_tables():
    half = QK_ROPE // 2
    inv_freq = ROPE_THETA ** (-jnp.arange(half, dtype=F32) / half)
    ang = jnp.arange(SEQ, dtype=F32)[:, None] * inv_freq[None, :]
    cos, sin = jnp.cos(ang), jnp.sin(ang)
    cos32 = jnp.concatenate([cos, cos], axis=1)
    sin32 = jnp.concatenate([-sin, sin], axis=1)
    pad = jnp.zeros((SEQ, HEAD_PAD - QK_DIM), F32)
    cosq = jnp.concatenate([jnp.ones((SEQ, QK_NOPE), F32), cos32, pad], axis=1)
    sinq = jnp.concatenate([jnp.zeros((SEQ, QK_NOPE), F32), sin32, pad], axis=1)
    return cosq, sinq, cos32.T, sin32.T


def kernel(x, norm_g, w_in, q_latent_g, kv_latent_g, w_uq, w_ukv, q_head_g, k_head_g,
           w_attn_proj, w_fourier_proj, b_merge, w_out):
    fc, stage_a, stage_bc, stage_bs = _dft_tables()
    own, own_mask, partner, par_mask = _head_column_maps()
    cosq, sinq, cost, sint = _rope_tables()

    lane = np.arange(N_HEADS * HEAD_PAD)
    head = lane // HEAD_PAD
    within = lane % HEAD_PAD
    v_slot = np.where(head % 2 == 0, within < V_DIM, within >= V_DIM)
    v_src = head * HEAD_PAD + QK_NOPE + (within % V_DIM)
    vones = jnp.asarray(np.where(v_slot, 0.0, 1.0)[None, :], F32)
    q_src = (head * QK_DIM)
    kn_rows = (np.arange(N_HEADS * QK_NOPE) // QK_NOPE) * HEAD_PAD + np.arange(N_HEADS * QK_NOPE) % QK_NOPE

    c0 = Q_RANK + KV_RANK + QK_ROPE
    x2 = x.reshape(TOKENS, D_MODEL)
    for l in range(DEPTH):
        wl = w_in[l]
        lat_pad = jnp.zeros((D_MODEL, LAT_WIDTH - c0), F32)
        u0 = c0 + ATTN_WIDTH
        w1 = jnp.concatenate([wl[:, :c0], lat_pad, wl[:, u0:u0 + FOURIER_WIDTH]], axis=1).astype(BF16)
        g0 = u0 + 2 * FOURIER_WIDTH
        wg = jnp.concatenate([wl[:, c0:u0], wl[:, u0 + FOURIER_WIDTH:g0], wl[:, g0:]],
                             axis=1).astype(BF16)

        wq = (w_uq[l][:, q_src + np.tile(own, N_HEADS)] * np.tile(own_mask, N_HEADS)).astype(BF16)
        wqp = (w_uq[l][:, q_src + np.tile(partner, N_HEADS)] * np.tile(par_mask, N_HEADS)).astype(BF16)
        wknT = w_ukv[l][:, kn_rows].T.astype(BF16)
        wv = (w_ukv[l][:, v_src] * v_slot.astype(np.float32)).astype(BF16)

        gq = q_head_g[l]
        gk = k_head_g[l]
        gq_pad = (gq[own] * own_mask)[None, :]
        gqp_pad = (gq[partner] * par_mask)[None, :]
        half = QK_ROPE // 2
        gk_rope = gk[QK_NOPE:]
        gkp = jnp.concatenate([gk_rope[half:], gk_rope[:half]])[:, None]

        q, kt, v, wr, wi = _proj_call(
            x2, norm_g[l][None, :], w1, q_latent_g[l][None, :], kv_latent_g[l][None, :],
            wq, wqp, wknT, wv, vones, gq_pad, gqp_pad, gk[:, None], gkp,
            cosq, sinq, cost, sint, fc)

        oa = _attn_call(q.reshape(BATCH, SEQ, N_HEADS * HEAD_PAD), kt,
                        v.reshape(BATCH, SEQ, N_HEADS * HEAD_PAD))

        width = RADIX * FOURIER_WIDTH
        xr, xi = _dft_a_call(wr.reshape(BATCH, RADIX, width), wi.reshape(BATCH, RADIX, width),
                             stage_a)
        f = _dft_b_call(xr.reshape(BATCH, RADIX, RADIX, FOURIER_WIDTH),
                        xi.reshape(BATCH, RADIX, RADIX, FOURIER_WIDTH), stage_bc, stage_bs)

        x2 = _merge_call(x2, oa.reshape(TOKENS, ATTN_WIDTH), f.reshape(TOKENS, FOURIER_WIDTH),
                         norm_g[l][None, :], wg, w_attn_proj[l].astype(BF16),
                         w_fourier_proj[l].astype(BF16), b_merge[l], w_out[l].astype(BF16))
    return x2.reshape(BATCH, SEQ, D_MODEL)
```

```python
import math

import numpy as np
import jax
import jax.numpy as jnp
from jax import lax
from jax.experimental import pallas as pl
from jax.experimental.pallas import tpu as pltpu

D_MODEL = 1024
BATCH = 4
SEQ = 4096
DEPTH = 4
N_HEADS = 8
QK_NOPE = 64
QK_ROPE = 32
QK_DIM = QK_NOPE + QK_ROPE
V_DIM = 64
HEAD_PAD = 128
ATTN_WIDTH = N_HEADS * V_DIM
Q_RANK = 256
KV_RANK = 256
ROPE_THETA = 10000.0
FOURIER_WIDTH = 512
GROUP_DIM = 128
N_GROUPS = 4
NORM_EPS = 1e-6
RADIX = 64
LAT_WIDTH = 640

F32 = jnp.float32
BF16 = jnp.bfloat16

VMEM_LIMIT = 48 * 1024 * 1024

TOKENS = BATCH * SEQ
PROJ_TM = 1024
ATTN_TQ = 256
MERGE_TM = 512

S2_PER_TILE = PROJ_TM // RADIX
N_HALVES = 2
GROUPS_PER_HALF = N_GROUPS // N_HALVES
HALF_WIDTH = FOURIER_WIDTH // N_HALVES
HALF_LANES = RADIX * HALF_WIDTH
PITCH = RADIX + 8
DFT_S1_CHUNK = 8

Q_SCALE = (QK_DIM ** -0.5) * math.log2(math.e)


def _dot(a, b):
    return jnp.dot(a, b, preferred_element_type=F32)


def _rms(x, axis):
    return lax.rsqrt(jnp.mean(x * x, axis=axis, keepdims=True) + NORM_EPS)


def _proj_body(x_ref, ng_ref, w1_ref, qlg_ref, kvlg_ref, wq_ref, wqp_ref, wknT_ref, wv_ref,
               vones_ref, gq_ref, gqp_ref, gk_ref, gkp_ref, cosq_ref, sinq_ref, cost_ref,
               sint_ref, fc_ref, q_out, kt_out, v_out, wr_out, wi_out, wsr_ref, wsi_ref):
    x = x_ref[...]
    hb = (x * _rms(x, -1) * ng_ref[...]).astype(BF16)
    p = _dot(hb, w1_ref[...])
    cq = p[:, 0:Q_RANK]
    ckv = p[:, Q_RANK:Q_RANK + KV_RANK]
    kpe = p[:, 512:LAT_WIDTH]
    u = p[:, LAT_WIDTH:LAT_WIDTH + FOURIER_WIDTH]

    ch_scale = GROUP_DIM ** -0.5
    fc = fc_ref[...].astype(BF16)
    for g in range(N_GROUPS):
        ug = u[:, g * GROUP_DIM:(g + 1) * GROUP_DIM].astype(BF16)
        wg = _dot(ug, fc) * ch_scale
        for t in range(S2_PER_TILE):
            rows = slice(t * RADIX, (t + 1) * RADIX)
            wsr_ref[g, t * PITCH:t * PITCH + RADIX, :] = wg[rows, :GROUP_DIM]
            wsi_ref[g, t * PITCH:t * PITCH + RADIX, :] = wg[rows, GROUP_DIM:]
    for s1 in range(RADIX):
        for g in range(N_GROUPS):
            half, j = divmod(g, GROUPS_PER_HALF)
            off = half * HALF_LANES + s1 * HALF_WIDTH + j * GROUP_DIM
            pick = pl.ds(s1, S2_PER_TILE, stride=PITCH)
            wr_out[0, :, off:off + GROUP_DIM] = wsr_ref[g, pick, :].astype(BF16)
            wi_out[0, :, off:off + GROUP_DIM] = wsi_ref[g, pick, :].astype(BF16)

    cqn = (cq * _rms(cq, -1) * qlg_ref[...]).astype(BF16)
    ckvn_f = ckv * _rms(ckv, -1) * kvlg_ref[...]
    ckvn = ckvn_f.astype(BF16)

    v_out[...] = (_dot(ckvn, wv_ref[...]) + vones_ref[...]).astype(BF16)

    q_raw = _dot(cqn, wq_ref[...])
    q_par = _dot(cqn, wqp_ref[...])
    gcq = gq_ref[...] * cosq_ref[...]
    gsq = gqp_ref[...] * sinq_ref[...]
    for h in range(N_HEADS):
        sl = slice(h * HEAD_PAD, (h + 1) * HEAD_PAD)
        qh = q_raw[:, sl]
        r = lax.rsqrt(jnp.sum(qh * qh, axis=-1, keepdims=True) * (1.0 / QK_DIM) + NORM_EPS)
        q_out[:, sl] = ((qh * gcq + q_par[:, sl] * gsq) * (r * Q_SCALE)).astype(BF16)

    ckvn_t = ckvn_f.T.astype(BF16)
    kn_t = _dot(wknT_ref[...], ckvn_t)
    kp = kpe.T[0:QK_ROPE, :]
    half = QK_ROPE // 2
    kp_sw = jnp.concatenate([kp[half:], kp[:half]], axis=0)
    gk = gk_ref[...]
    rope = kp * (gk[QK_NOPE:] * cost_ref[...]) + kp_sw * (gkp_ref[...] * sint_ref[...])
    ss_pe = jnp.sum(kp * kp, axis=0, keepdims=True)
    zeros = jnp.zeros((HEAD_PAD - QK_DIM, kp.shape[1]), BF16)
    for h in range(N_HEADS):
        kn = kn_t[h * QK_NOPE:(h + 1) * QK_NOPE, :]
        ss = jnp.sum(kn * kn, axis=0, keepdims=True) + ss_pe
        r = lax.rsqrt(ss * (1.0 / QK_DIM) + NORM_EPS)
        base = h * HEAD_PAD
        kt_out[0, base:base + QK_NOPE, :] = (kn * gk[:QK_NOPE] * r).astype(BF16)
        kt_out[0, base + QK_NOPE:base + QK_DIM, :] = (rope * r).astype(BF16)
        kt_out[0, base + QK_DIM:base + HEAD_PAD, :] = zeros


def _proj_call(x2, ng, w1, qlg, kvlg, wq, wqp, wknT, wv, vones, gq, gqp, gk, gkp,
               cosq, sinq, cost, sint, fc):
    tm = PROJ_TM
    spb = SEQ // tm
    const = lambda shape: pl.BlockSpec(shape, lambda i: (0,) * len(shape))
    in_specs = [
        pl.BlockSpec((tm, D_MODEL), lambda i: (i, 0)),
        const((1, D_MODEL)),
        const((D_MODEL, LAT_WIDTH + FOURIER_WIDTH)),
        const((1, Q_RANK)),
        const((1, KV_RANK)),
        const((Q_RANK, N_HEADS * HEAD_PAD)),
        const((Q_RANK, N_HEADS * HEAD_PAD)),
        const((N_HEADS * QK_NOPE, KV_RANK)),
        const((KV_RANK, N_HEADS * HEAD_PAD)),
        const((1, N_HEADS * HEAD_PAD)),
        const((1, HEAD_PAD)),
        const((1, HEAD_PAD)),
        const((QK_DIM, 1)),
        const((QK_ROPE, 1)),
        pl.BlockSpec((tm, HEAD_PAD), lambda i: (i % spb, 0)),
        pl.BlockSpec((tm, HEAD_PAD), lambda i: (i % spb, 0)),
        pl.BlockSpec((QK_ROPE, tm), lambda i: (0, i % spb)),
        pl.BlockSpec((QK_ROPE, tm), lambda i: (0, i % spb)),
        const((GROUP_DIM, 2 * GROUP_DIM)),
    ]
    out_shape = [
        jax.ShapeDtypeStruct((TOKENS, N_HEADS * HEAD_PAD), BF16),
        jax.ShapeDtypeStruct((BATCH, N_HEADS * HEAD_PAD, SEQ), BF16),
        jax.ShapeDtypeStruct((TOKENS, N_HEADS * HEAD_PAD), BF16),
        jax.ShapeDtypeStruct((BATCH, RADIX, N_HALVES * HALF_LANES), BF16),
        jax.ShapeDtypeStruct((BATCH, RADIX, N_HALVES * HALF_LANES), BF16),
    ]
    dft_in_spec = pl.BlockSpec((1, S2_PER_TILE, N_HALVES * HALF_LANES),
                               lambda i: (i // spb, i % spb, 0))
    out_specs = [
        pl.BlockSpec((tm, N_HEADS * HEAD_PAD), lambda i: (i, 0)),
        pl.BlockSpec((1, N_HEADS * HEAD_PAD, tm), lambda i: (i // spb, 0, i % spb)),
        pl.BlockSpec((tm, N_HEADS * HEAD_PAD), lambda i: (i, 0)),
        dft_in_spec,
        dft_in_spec,
    ]
    regroup = pltpu.VMEM((N_GROUPS, S2_PER_TILE * PITCH, GROUP_DIM), F32)
    return pl.pallas_call(
        _proj_body,
        grid=(TOKENS // tm,),
        in_specs=in_specs,
        out_specs=out_specs,
        out_shape=out_shape,
        scratch_shapes=[regroup, regroup],
        compiler_params=pltpu.CompilerParams(
            dimension_semantics=("arbitrary",), vmem_limit_bytes=VMEM_LIMIT),
        name="proj",
    )(x2, ng, w1, qlg, kvlg, wq, wqp, wknT, wv, vones, gq, gqp, gk, gkp,
      cosq, sinq, cost, sint, fc)


def _attn_body(q_ref, kt_ref, v_ref, o_ref):
    outs = []
    for hh in range(2):
        sl = slice(hh * HEAD_PAD, (hh + 1) * HEAD_PAD)
        s = _dot(q_ref[0, :, sl], kt_ref[0, sl, :])
        m = jnp.max(s, axis=-1, keepdims=True)
        p = jnp.exp2(s - m).astype(BF16)
        outs.append(_dot(p, v_ref[0, :, sl]))
    even, odd = outs
    lane = lax.broadcasted_iota(jnp.int32, even.shape, 1)
    num = jnp.where(lane < V_DIM, even, odd)
    den = jnp.where(lane < V_DIM, pltpu.roll(even, V_DIM, 1), pltpu.roll(odd, V_DIM, 1))
    o_ref[0] = (num / den).astype(BF16)


def _attn_call(q3, kt, v3):
    tq = ATTN_TQ
    return pl.pallas_call(
        _attn_body,
        grid=(BATCH, N_HEADS // 2, SEQ // tq),
        in_specs=[
            pl.BlockSpec((1, tq, 2 * HEAD_PAD), lambda b, j, i: (b, i, j)),
            pl.BlockSpec((1, 2 * HEAD_PAD, SEQ), lambda b, j, i: (b, j, 0)),
            pl.BlockSpec((1, SEQ, 2 * HEAD_PAD), lambda b, j, i: (b, 0, j)),
        ],
        out_specs=pl.BlockSpec((1, tq, 2 * V_DIM), lambda b, j, i: (b, i, j)),
        out_shape=jax.ShapeDtypeStruct((BATCH, SEQ, ATTN_WIDTH), BF16),
        compiler_params=pltpu.CompilerParams(
            dimension_semantics=("arbitrary", "arbitrary", "arbitrary"),
            vmem_limit_bytes=VMEM_LIMIT),
        name="attn",
    )(q3, kt, v3)


def _dft_body(wr_ref, wi_ref, cs_ref, tb_ref, f_out, xr_ref, xi_ref, y_ref):
    cs = cs_ref[...].astype(BF16)
    for c in range(RADIX // DFT_S1_CHUNK):
        lanes = slice(c * DFT_S1_CHUNK * HALF_WIDTH, (c + 1) * DFT_S1_CHUNK * HALF_WIDTH)
        a = _dot(cs, wr_ref[0, :, lanes])
        b = _dot(cs, wi_ref[0, :, lanes])
        xr = a[:RADIX] + b[RADIX:]
        xi = b[:RADIX] - a[RADIX:]
        for t in range(DFT_S1_CHUNK):
            s1 = c * DFT_S1_CHUNK + t
            for j in range(GROUPS_PER_HALF):
                sub = slice(t * HALF_WIDTH + j * GROUP_DIM, t * HALF_WIDTH + (j + 1) * GROUP_DIM)
                xr_ref[j, s1 * PITCH:s1 * PITCH + RADIX, :] = xr[:, sub]
                xi_ref[j, s1 * PITCH:s1 * PITCH + RADIX, :] = xi[:, sub]
    for k2 in range(RADIX):
        pick = pl.ds(k2, RADIX, stride=PITCH)
        zr = jnp.concatenate([xr_ref[j, pick, :] for j in range(GROUPS_PER_HALF)], axis=1)
        zi = jnp.concatenate([xi_ref[j, pick, :] for j in range(GROUPS_PER_HALF)], axis=1)
        z = jnp.concatenate([zr, zi], axis=0).astype(BF16)
        y = _dot(tb_ref[k2].astype(BF16), z)
        for j in range(GROUPS_PER_HALF):
            y_ref[j, pick, :] = y[:, j * GROUP_DIM:(j + 1) * GROUP_DIM]
    for k1 in range(RADIX):
        rows = slice(k1 * PITCH, k1 * PITCH + RADIX)
        for j in range(GROUPS_PER_HALF):
            f_out[0, k1, :, j * GROUP_DIM:(j + 1) * GROUP_DIM] = y_ref[j, rows, :].astype(BF16)


def _dft_call(wr, wi, cs, tb):
    spec = pl.BlockSpec((1, RADIX, HALF_LANES), lambda b, h: (b, 0, h))
    regroup = pltpu.VMEM((GROUPS_PER_HALF, RADIX * PITCH, GROUP_DIM), F32)
    return pl.pallas_call(
        _dft_body,
        grid=(BATCH, N_HALVES),
        in_specs=[spec, spec,
                  pl.BlockSpec((2 * RADIX, RADIX), lambda b, h: (0, 0)),
                  pl.BlockSpec((RADIX, RADIX, 2 * RADIX), lambda b, h: (0, 0, 0))],
        out_specs=pl.BlockSpec((1, RADIX, RADIX, HALF_WIDTH), lambda b, h: (b, 0, 0, h)),
        out_shape=jax.ShapeDtypeStruct((BATCH, RADIX, RADIX, FOURIER_WIDTH), BF16),
        scratch_shapes=[regroup, regroup, regroup],
        compiler_params=pltpu.CompilerParams(
            dimension_semantics=("arbitrary", "arbitrary"), vmem_limit_bytes=VMEM_LIMIT),
        name="dft",
    )(wr, wi, cs, tb)


def _merge_body(x_ref, oa_ref, f_ref, ng_ref, wg_ref, wa_ref, wf_ref, bm_ref, wo_ref, out_ref):
    x = x_ref[...]
    hb = (x * _rms(x, -1) * ng_ref[...]).astype(BF16)
    gates = _dot(hb, wg_ref[...])
    z_a = gates[:, 0:ATTN_WIDTH]
    z_f = gates[:, ATTN_WIDTH:ATTN_WIDTH + FOURIER_WIDTH]
    g_a = gates[:, 1024:1024 + D_MODEL]
    g_f = gates[:, 1024 + D_MODEL:]
    ya = _dot((oa_ref[...].astype(F32) * (z_a * jax.nn.sigmoid(z_a))).astype(BF16), wa_ref[...])
    yf = _dot((f_ref[...].astype(F32) * (z_f * jax.nn.sigmoid(z_f))).astype(BF16), wf_ref[...])
    bm = bm_ref[...]
    m = jax.nn.sigmoid(g_a + bm[0:1]) * ya + jax.nn.sigmoid(g_f + bm[1:2]) * yf
    out_ref[...] = x + _dot(m.astype(BF16), wo_ref[...])


def _merge_call(x2, oa, f, ng, wg, wa, wf, bm, wo):
    tm = MERGE_TM
    const = lambda shape: pl.BlockSpec(shape, lambda i: (0,) * len(shape))
    return pl.pallas_call(
        _merge_body,
        grid=(TOKENS // tm,),
        in_specs=[
            pl.BlockSpec((tm, D_MODEL), lambda i: (i, 0)),
            pl.BlockSpec((tm, ATTN_WIDTH), lambda i: (i, 0)),
            pl.BlockSpec((tm, FOURIER_WIDTH), lambda i: (i, 0)),
            const((1, D_MODEL)),
            const((D_MODEL, 3 * D_MODEL)),
            const((ATTN_WIDTH, D_MODEL)),
            const((FOURIER_WIDTH, D_MODEL)),
            const((2, D_MODEL)),
            const((D_MODEL, D_MODEL)),
        ],
        out_specs=pl.BlockSpec((tm, D_MODEL), lambda i: (i, 0)),
        out_shape=jax.ShapeDtypeStruct((TOKENS, D_MODEL), F32),
        compiler_params=pltpu.CompilerParams(
            dimension_semantics=("arbitrary",), vmem_limit_bytes=VMEM_LIMIT),
        name="merge",
    )(x2, oa, f, ng, wg, wa, wf, bm, wo)


def _dft_tables():
    def cs(num, den):
        ang = 2.0 * np.pi * (num % den).astype(np.float64) / den
        return np.cos(ang), np.sin(ang)

    c = np.arange(GROUP_DIM)
    cc, sc = cs(np.outer(c, c), GROUP_DIM)
    fc = np.concatenate([cc, -sc], axis=1)

    k = np.arange(RADIX)
    ca, sa = cs(np.outer(k, k), RADIX)
    stage_a = np.concatenate([ca, sa], axis=0) / 8.0

    k2 = k[:, None, None]
    k1 = k[None, :, None]
    s1 = k[None, None, :]
    cb, sb = cs(s1 * (RADIX * k1 + k2), SEQ)
    stage_b = np.concatenate([cb, sb], axis=2) / 8.0
    return jnp.asarray(fc, F32), jnp.asarray(stage_a, F32), jnp.asarray(stage_b, F32)


def _head_column_maps():
    lanes = np.arange(HEAD_PAD)
    own = np.where(lanes < QK_DIM, lanes, 0)
    own_mask = (lanes < QK_DIM).astype(np.float32)
    half = QK_ROPE // 2
    partner = np.where(lanes < QK_NOPE + half, lanes + half, lanes - half)
    par_mask = ((lanes >= QK_NOPE) & (lanes < QK_DIM)).astype(np.float32)
    partner = np.where(par_mask > 0, partner, 0)
    return own, own_mask, partner, par_mask


def _rope_tables():
    half = QK_ROPE // 2
    inv_freq = ROPE_THETA ** (-jnp.arange(half, dtype=F32) / half)
    ang = jnp.arange(SEQ, dtype=F32)[:, None] * inv_freq[None, :]
    cos, sin = jnp.cos(ang), jnp.sin(ang)
    cos32 = jnp.concatenate([cos, cos], axis=1)
    sin32 = jnp.concatenate([-sin, sin], axis=1)
    pad = jnp.zeros((SEQ, HEAD_PAD - QK_DIM), F32)
    cosq = jnp.concatenate([jnp.ones((SEQ, QK_NOPE), F32), cos32, pad], axis=1)
    sinq = jnp.concatenate([jnp.zeros((SEQ, QK_NOPE), F32), sin32, pad], axis=1)
    return cosq, sinq, cos32.T, sin32.T


def kernel(x, norm_g, w_in, q_latent_g, kv_latent_g, w_uq, w_ukv, q_head_g, k_head_g,
           w_attn_proj, w_fourier_proj, b_merge, w_out):
    fc, stage_a, stage_b = _dft_tables()
    own, own_mask, partner, par_mask = _head_column_maps()
    cosq, sinq, cost, sint = _rope_tables()

    lane = np.arange(N_HEADS * HEAD_PAD)
    head = lane // HEAD_PAD
    within = lane % HEAD_PAD
    v_slot = np.where(head % 2 == 0, within < V_DIM, within >= V_DIM)
    v_src = head * HEAD_PAD + QK_NOPE + (within % V_DIM)
    vones = jnp.asarray(np.where(v_slot, 0.0, 1.0)[None, :], F32)
    q_src = (head * QK_DIM)
    kn_rows = (np.arange(N_HEADS * QK_NOPE) // QK_NOPE) * HEAD_PAD + np.arange(N_HEADS * QK_NOPE) % QK_NOPE

    c0 = Q_RANK + KV_RANK + QK_ROPE
    x2 = x.reshape(TOKENS, D_MODEL)
    for l in range(DEPTH):
        wl = w_in[l]
        lat_pad = jnp.zeros((D_MODEL, LAT_WIDTH - c0), F32)
        u0 = c0 + ATTN_WIDTH
        w1 = jnp.concatenate([wl[:, :c0], lat_pad, wl[:, u0:u0 + FOURIER_WIDTH]], axis=1).astype(BF16)
        g0 = u0 + 2 * FOURIER_WIDTH
        wg = jnp.concatenate([wl[:, c0:u0], wl[:, u0 + FOURIER_WIDTH:g0], wl[:, g0:]],
                             axis=1).astype(BF16)

        wq = (w_uq[l][:, q_src + np.tile(own, N_HEADS)] * np.tile(own_mask, N_HEADS)).astype(BF16)
        wqp = (w_uq[l][:, q_src + np.tile(partner, N_HEADS)] * np.tile(par_mask, N_HEADS)).astype(BF16)
        wknT = w_ukv[l][:, kn_rows].T.astype(BF16)
        wv = (w_ukv[l][:, v_src] * v_slot.astype(np.float32)).astype(BF16)

        gq = q_head_g[l]
        gk = k_head_g[l]
        gq_pad = (gq[own] * own_mask)[None, :]
        gqp_pad = (gq[partner] * par_mask)[None, :]
        half = QK_ROPE // 2
        gk_rope = gk[QK_NOPE:]
        gkp = jnp.concatenate([gk_rope[half:], gk_rope[:half]])[:, None]

        q, kt, v, wr, wi = _proj_call(
            x2, norm_g[l][None, :], w1, q_latent_g[l][None, :], kv_latent_g[l][None, :],
            wq, wqp, wknT, wv, vones, gq_pad, gqp_pad, gk[:, None], gkp,
            cosq, sinq, cost, sint, fc)

        oa = _attn_call(q.reshape(BATCH, SEQ, N_HEADS * HEAD_PAD), kt,
                        v.reshape(BATCH, SEQ, N_HEADS * HEAD_PAD))

        f = _dft_call(wr, wi, stage_a, stage_b)

        x2 = _merge_call(x2, oa.reshape(TOKENS, ATTN_WIDTH), f.reshape(TOKENS, FOURIER_WIDTH),
                         norm_g[l][None, :], wg, w_attn_proj[l].astype(BF16),
                         w_fourier_proj[l].astype(BF16), b_merge[l], w_out[l].astype(BF16))
    return x2.reshape(BATCH, SEQ, D_MODEL)
```

```python
import math

import numpy as np
import jax
import jax.numpy as jnp
from jax import lax
from jax.experimental import pallas as pl
from jax.experimental.pallas import tpu as pltpu

D_MODEL = 1024
BATCH = 4
SEQ = 4096
DEPTH = 4
N_HEADS = 8
QK_NOPE = 64
QK_ROPE = 32
QK_DIM = QK_NOPE + QK_ROPE
V_DIM = 64
HEAD_PAD = 128
ATTN_WIDTH = N_HEADS * V_DIM
Q_RANK = 256
KV_RANK = 256
ROPE_THETA = 10000.0
FOURIER_WIDTH = 512
GROUP_DIM = 128
N_GROUPS = 4
NORM_EPS = 1e-6
RADIX = 64
LAT_WIDTH = 640

F32 = jnp.float32
BF16 = jnp.bfloat16

VMEM_LIMIT = 48 * 1024 * 1024

TOKENS = BATCH * SEQ
PROJ_TM = 1024
ATTN_TQ = 256
ATTN_HEADS = 4
MERGE_TM = 512

S2_PER_TILE = PROJ_TM // RADIX
N_HALVES = 2
GROUPS_PER_HALF = N_GROUPS // N_HALVES
HALF_WIDTH = FOURIER_WIDTH // N_HALVES
HALF_LANES = RADIX * HALF_WIDTH
PITCH = RADIX + 8
DFT_S1_CHUNK = 8

Q_SCALE = (QK_DIM ** -0.5) * math.log2(math.e)


def _dot(a, b):
    return jnp.dot(a, b, preferred_element_type=F32)


def _rms(x, axis):
    return lax.rsqrt(jnp.mean(x * x, axis=axis, keepdims=True) + NORM_EPS)


def _proj_body(x_ref, ng_ref, w1_ref, qlg_ref, kvlg_ref, wq_ref, wqp_ref, wknT_ref, wv_ref,
               vones_ref, gq_ref, gqp_ref, gk_ref, gkp_ref, cosq_ref, sinq_ref, cost_ref,
               sint_ref, fc_ref, q_out, kt_out, v_out, wr_out, wi_out, wsr_ref, wsi_ref):
    x = x_ref[...]
    hb = (x * _rms(x, -1) * ng_ref[...]).astype(BF16)
    p = _dot(hb, w1_ref[...])
    cq = p[:, 0:Q_RANK]
    ckv = p[:, Q_RANK:Q_RANK + KV_RANK]
    kpe = p[:, 512:LAT_WIDTH]
    u = p[:, LAT_WIDTH:LAT_WIDTH + FOURIER_WIDTH]

    ch_scale = GROUP_DIM ** -0.5
    fc = fc_ref[...].astype(BF16)
    for g in range(N_GROUPS):
        ug = u[:, g * GROUP_DIM:(g + 1) * GROUP_DIM].astype(BF16)
        wg = _dot(ug, fc) * ch_scale
        for t in range(S2_PER_TILE):
            rows = slice(t * RADIX, (t + 1) * RADIX)
            wsr_ref[g, t * PITCH:t * PITCH + RADIX, :] = wg[rows, :GROUP_DIM]
            wsi_ref[g, t * PITCH:t * PITCH + RADIX, :] = wg[rows, GROUP_DIM:]
    for s1 in range(RADIX):
        for g in range(N_GROUPS):
            half, j = divmod(g, GROUPS_PER_HALF)
            off = half * HALF_LANES + s1 * HALF_WIDTH + j * GROUP_DIM
            pick = pl.ds(s1, S2_PER_TILE, stride=PITCH)
            wr_out[0, :, off:off + GROUP_DIM] = wsr_ref[g, pick, :].astype(BF16)
            wi_out[0, :, off:off + GROUP_DIM] = wsi_ref[g, pick, :].astype(BF16)

    cqn = (cq * _rms(cq, -1) * qlg_ref[...]).astype(BF16)
    ckvn_f = ckv * _rms(ckv, -1) * kvlg_ref[...]
    ckvn = ckvn_f.astype(BF16)

    v_out[...] = (_dot(ckvn, wv_ref[...]) + vones_ref[...]).astype(BF16)

    q_raw = _dot(cqn, wq_ref[...])
    q_par = _dot(cqn, wqp_ref[...])
    gcq = gq_ref[...] * cosq_ref[...]
    gsq = gqp_ref[...] * sinq_ref[...]
    for h in range(N_HEADS):
        sl = slice(h * HEAD_PAD, (h + 1) * HEAD_PAD)
        qh = q_raw[:, sl]
        r = lax.rsqrt(jnp.sum(qh * qh, axis=-1, keepdims=True) * (1.0 / QK_DIM) + NORM_EPS)
        q_out[:, sl] = ((qh * gcq + q_par[:, sl] * gsq) * (r * Q_SCALE)).astype(BF16)

    ckvn_t = ckvn_f.T.astype(BF16)
    kn_t = _dot(wknT_ref[...], ckvn_t)
    kp = kpe.T[0:QK_ROPE, :]
    half = QK_ROPE // 2
    kp_sw = jnp.concatenate([kp[half:], kp[:half]], axis=0)
    gk = gk_ref[...]
    rope = kp * (gk[QK_NOPE:] * cost_ref[...]) + kp_sw * (gkp_ref[...] * sint_ref[...])
    ss_pe = jnp.sum(kp * kp, axis=0, keepdims=True)
    zeros = jnp.zeros((HEAD_PAD - QK_DIM, kp.shape[1]), BF16)
    for h in range(N_HEADS):
        kn = kn_t[h * QK_NOPE:(h + 1) * QK_NOPE, :]
        ss = jnp.sum(kn * kn, axis=0, keepdims=True) + ss_pe
        r = lax.rsqrt(ss * (1.0 / QK_DIM) + NORM_EPS)
        base = h * HEAD_PAD
        kt_out[0, base:base + QK_NOPE, :] = (kn * gk[:QK_NOPE] * r).astype(BF16)
        kt_out[0, base + QK_NOPE:base + QK_DIM, :] = (rope * r).astype(BF16)
        kt_out[0, base + QK_DIM:base + HEAD_PAD, :] = zeros


def _layer_spec(layer, shape):
    return pl.BlockSpec((None,) + shape, lambda *_: (layer,) + (0,) * len(shape))


def _proj_call(layer, x2, ng, w1, qlg, kvlg, wq, wqp, wknT, wv, vones, gq, gqp, gk, gkp,
               cosq, sinq, cost, sint, fc):
    tm = PROJ_TM
    spb = SEQ // tm
    const = lambda shape: pl.BlockSpec(shape, lambda i: (0,) * len(shape))
    per_layer = lambda *shape: _layer_spec(layer, shape)
    in_specs = [
        pl.BlockSpec((tm, D_MODEL), lambda i: (i, 0)),
        per_layer(1, D_MODEL),
        per_layer(D_MODEL, LAT_WIDTH + FOURIER_WIDTH),
        per_layer(1, Q_RANK),
        per_layer(1, KV_RANK),
        per_layer(Q_RANK, N_HEADS * HEAD_PAD),
        per_layer(Q_RANK, N_HEADS * HEAD_PAD),
        per_layer(N_HEADS * QK_NOPE, KV_RANK),
        per_layer(KV_RANK, N_HEADS * HEAD_PAD),
        const((1, N_HEADS * HEAD_PAD)),
        per_layer(1, HEAD_PAD),
        per_layer(1, HEAD_PAD),
        per_layer(QK_DIM, 1),
        per_layer(QK_ROPE, 1),
        pl.BlockSpec((tm, HEAD_PAD), lambda i: (i % spb, 0)),
        pl.BlockSpec((tm, HEAD_PAD), lambda i: (i % spb, 0)),
        pl.BlockSpec((QK_ROPE, tm), lambda i: (0, i % spb)),
        pl.BlockSpec((QK_ROPE, tm), lambda i: (0, i % spb)),
        const((GROUP_DIM, 2 * GROUP_DIM)),
    ]
    out_shape = [
        jax.ShapeDtypeStruct((TOKENS, N_HEADS * HEAD_PAD), BF16),
        jax.ShapeDtypeStruct((BATCH, N_HEADS * HEAD_PAD, SEQ), BF16),
        jax.ShapeDtypeStruct((TOKENS, N_HEADS * HEAD_PAD), BF16),
        jax.ShapeDtypeStruct((BATCH, RADIX, N_HALVES * HALF_LANES), BF16),
        jax.ShapeDtypeStruct((BATCH, RADIX, N_HALVES * HALF_LANES), BF16),
    ]
    dft_in_spec = pl.BlockSpec((1, S2_PER_TILE, N_HALVES * HALF_LANES),
                               lambda i: (i // spb, i % spb, 0))
    out_specs = [
        pl.BlockSpec((tm, N_HEADS * HEAD_PAD), lambda i: (i, 0)),
        pl.BlockSpec((1, N_HEADS * HEAD_PAD, tm), lambda i: (i // spb, 0, i % spb)),
        pl.BlockSpec((tm, N_HEADS * HEAD_PAD), lambda i: (i, 0)),
        dft_in_spec,
        dft_in_spec,
    ]
    regroup = pltpu.VMEM((N_GROUPS, S2_PER_TILE * PITCH, GROUP_DIM), F32)
    return pl.pallas_call(
        _proj_body,
        grid=(TOKENS // tm,),
        in_specs=in_specs,
        out_specs=out_specs,
        out_shape=out_shape,
        scratch_shapes=[regroup, regroup],
        compiler_params=pltpu.CompilerParams(
            dimension_semantics=("arbitrary",), vmem_limit_bytes=VMEM_LIMIT),
        name="proj",
    )(x2, ng, w1, qlg, kvlg, wq, wqp, wknT, wv, vones, gq, gqp, gk, gkp,
      cosq, sinq, cost, sint, fc)


def _attn_body(q_ref, kt_ref, v_ref, o_ref):
    outs = []
    for hh in range(ATTN_HEADS):
        sl = slice(hh * HEAD_PAD, (hh + 1) * HEAD_PAD)
        s = _dot(q_ref[0, :, sl], kt_ref[0, sl, :])
        m = jnp.max(s, axis=-1, keepdims=True)
        p = jnp.exp2(s - m).astype(BF16)
        outs.append(_dot(p, v_ref[0, :, sl]))
    lane = lax.broadcasted_iota(jnp.int32, outs[0].shape, 1)
    for pair in range(ATTN_HEADS // 2):
        even, odd = outs[2 * pair], outs[2 * pair + 1]
        num = jnp.where(lane < V_DIM, even, odd)
        den = jnp.where(lane < V_DIM, pltpu.roll(even, V_DIM, 1), pltpu.roll(odd, V_DIM, 1))
        o_ref[0, :, pair * HEAD_PAD:(pair + 1) * HEAD_PAD] = (num / den).astype(BF16)


def _attn_call(q3, kt, v3):
    tq = ATTN_TQ
    hps = ATTN_HEADS
    return pl.pallas_call(
        _attn_body,
        grid=(BATCH, N_HEADS // hps, SEQ // tq),
        in_specs=[
            pl.BlockSpec((1, tq, hps * HEAD_PAD), lambda b, j, i: (b, i, j)),
            pl.BlockSpec((1, hps * HEAD_PAD, SEQ), lambda b, j, i: (b, j, 0)),
            pl.BlockSpec((1, SEQ, hps * HEAD_PAD), lambda b, j, i: (b, 0, j)),
        ],
        out_specs=pl.BlockSpec((1, tq, hps * V_DIM), lambda b, j, i: (b, i, j)),
        out_shape=jax.ShapeDtypeStruct((BATCH, SEQ, ATTN_WIDTH), BF16),
        compiler_params=pltpu.CompilerParams(
            dimension_semantics=("arbitrary", "arbitrary", "arbitrary"),
            vmem_limit_bytes=VMEM_LIMIT),
        name="attn",
    )(q3, kt, v3)


def _dft_body(wr_ref, wi_ref, cs_ref, tb_ref, f_out, xr_ref, xi_ref, y_ref):
    cs = cs_ref[...].astype(BF16)
    for c in range(RADIX // DFT_S1_CHUNK):
        lanes = slice(c * DFT_S1_CHUNK * HALF_WIDTH, (c + 1) * DFT_S1_CHUNK * HALF_WIDTH)
        a = _dot(cs, wr_ref[0, :, lanes])
        b = _dot(cs, wi_ref[0, :, lanes])
        xr = a[:RADIX] + b[RADIX:]
        xi = b[:RADIX] - a[RADIX:]
        for t in range(DFT_S1_CHUNK):
            s1 = c * DFT_S1_CHUNK + t
            for j in range(GROUPS_PER_HALF):
                sub = slice(t * HALF_WIDTH + j * GROUP_DIM, t * HALF_WIDTH + (j + 1) * GROUP_DIM)
                xr_ref[j, s1 * PITCH:s1 * PITCH + RADIX, :] = xr[:, sub]
                xi_ref[j, s1 * PITCH:s1 * PITCH + RADIX, :] = xi[:, sub]
    for k2 in range(RADIX):
        pick = pl.ds(k2, RADIX, stride=PITCH)
        zr = jnp.concatenate([xr_ref[j, pick, :] for j in range(GROUPS_PER_HALF)], axis=1)
        zi = jnp.concatenate([xi_ref[j, pick, :] for j in range(GROUPS_PER_HALF)], axis=1)
        z = jnp.concatenate([zr, zi], axis=0).astype(BF16)
        y = _dot(tb_ref[k2].astype(BF16), z)
        for j in range(GROUPS_PER_HALF):
            y_ref[j, pick, :] = y[:, j * GROUP_DIM:(j + 1) * GROUP_DIM]
    for k1 in range(RADIX):
        rows = slice(k1 * PITCH, k1 * PITCH + RADIX)
        for j in range(GROUPS_PER_HALF):
            f_out[0, k1, :, j * GROUP_DIM:(j + 1) * GROUP_DIM] = y_ref[j, rows, :].astype(BF16)


def _dft_call(wr, wi, cs, tb):
    spec = pl.BlockSpec((1, RADIX, HALF_LANES), lambda b, h: (b, 0, h))
    regroup = pltpu.VMEM((GROUPS_PER_HALF, RADIX * PITCH, GROUP_DIM), F32)
    return pl.pallas_call(
        _dft_body,
        grid=(BATCH, N_HALVES),
        in_specs=[spec, spec,
                  pl.BlockSpec((2 * RADIX, RADIX), lambda b, h: (0, 0)),
                  pl.BlockSpec((RADIX, RADIX, 2 * RADIX), lambda b, h: (0, 0, 0))],
        out_specs=pl.BlockSpec((1, RADIX, RADIX, HALF_WIDTH), lambda b, h: (b, 0, 0, h)),
        out_shape=jax.ShapeDtypeStruct((BATCH, RADIX, RADIX, FOURIER_WIDTH), BF16),
        scratch_shapes=[regroup, regroup, regroup],
        compiler_params=pltpu.CompilerParams(
            dimension_semantics=("arbitrary", "arbitrary"), vmem_limit_bytes=VMEM_LIMIT),
        name="dft",
    )(wr, wi, cs, tb)


def _merge_body(x_ref, oa_ref, f_ref, ng_ref, wg_ref, wa_ref, wf_ref, bm_ref, wo_ref, out_ref):
    x = x_ref[...]
    hb = (x * _rms(x, -1) * ng_ref[...]).astype(BF16)
    gates = _dot(hb, wg_ref[...])
    z_a = gates[:, 0:ATTN_WIDTH]
    z_f = gates[:, ATTN_WIDTH:ATTN_WIDTH + FOURIER_WIDTH]
    g_a = gates[:, 1024:1024 + D_MODEL]
    g_f = gates[:, 1024 + D_MODEL:]
    ya = _dot((oa_ref[...].astype(F32) * (z_a * jax.nn.sigmoid(z_a))).astype(BF16), wa_ref[...])
    yf = _dot((f_ref[...].astype(F32) * (z_f * jax.nn.sigmoid(z_f))).astype(BF16), wf_ref[...])
    bm = bm_ref[...]
    m = jax.nn.sigmoid(g_a + bm[0:1]) * ya + jax.nn.sigmoid(g_f + bm[1:2]) * yf
    out_ref[...] = x + _dot(m.astype(BF16), wo_ref[...])


def _merge_call(layer, x2, oa, f, ng, wg, wa, wf, bm, wo):
    tm = MERGE_TM
    per_layer = lambda *shape: _layer_spec(layer, shape)
    return pl.pallas_call(
        _merge_body,
        grid=(TOKENS // tm,),
        in_specs=[
            pl.BlockSpec((tm, D_MODEL), lambda i: (i, 0)),
            pl.BlockSpec((tm, ATTN_WIDTH), lambda i: (i, 0)),
            pl.BlockSpec((tm, FOURIER_WIDTH), lambda i: (i, 0)),
            per_layer(1, D_MODEL),
            per_layer(D_MODEL, 3 * D_MODEL),
            per_layer(ATTN_WIDTH, D_MODEL),
            per_layer(FOURIER_WIDTH, D_MODEL),
            per_layer(2, D_MODEL),
            per_layer(D_MODEL, D_MODEL),
        ],
        out_specs=pl.BlockSpec((tm, D_MODEL), lambda i: (i, 0)),
        out_shape=jax.ShapeDtypeStruct((TOKENS, D_MODEL), F32),
        compiler_params=pltpu.CompilerParams(
            dimension_semantics=("arbitrary",), vmem_limit_bytes=VMEM_LIMIT),
        name="merge",
    )(x2, oa, f, ng, wg, wa, wf, bm, wo)


def _dft_tables():
    def cs(num, den):
        ang = 2.0 * np.pi * (num % den).astype(np.float64) / den
        return np.cos(ang), np.sin(ang)

    c = np.arange(GROUP_DIM)
    cc, sc = cs(np.outer(c, c), GROUP_DIM)
    fc = np.concatenate([cc, -sc], axis=1)

    k = np.arange(RADIX)
    ca, sa = cs(np.outer(k, k), RADIX)
    stage_a = np.concatenate([ca, sa], axis=0) / 8.0

    k2 = k[:, None, None]
    k1 = k[None, :, None]
    s1 = k[None, None, :]
    cb, sb = cs(s1 * (RADIX * k1 + k2), SEQ)
    stage_b = np.concatenate([cb, sb], axis=2) / 8.0
    return jnp.asarray(fc, F32), jnp.asarray(stage_a, F32), jnp.asarray(stage_b, F32)


def _rope_tables():
    half = QK_ROPE // 2
    inv_freq = ROPE_THETA ** (-np.arange(half, dtype=np.float64) / half)
    ang = np.arange(SEQ, dtype=np.float64)[:, None] * inv_freq[None, :]
    cos, sin = np.cos(ang), np.sin(ang)
    cos32 = np.concatenate([cos, cos], axis=1)
    sin32 = np.concatenate([-sin, sin], axis=1)
    pad = np.zeros((SEQ, HEAD_PAD - QK_DIM))
    cosq = np.concatenate([np.ones((SEQ, QK_NOPE)), cos32, pad], axis=1)
    sinq = np.concatenate([np.zeros((SEQ, QK_NOPE)), sin32, pad], axis=1)
    return tuple(jnp.asarray(t, F32) for t in (cosq, sinq, cos32.T, sin32.T))


def _swap_halves(a):
    half = a.shape[-1] // 2
    return jnp.concatenate([a[..., half:], a[..., :half]], axis=-1)


def _pad_last(a, before, after):
    return jnp.pad(a, [(0, 0)] * (a.ndim - 1) + [(before, after)])


def _prepare_params(norm_g, w_in, q_latent_g, kv_latent_g, w_uq, w_ukv, q_head_g, k_head_g,
                    w_attn_proj, w_fourier_proj, b_merge, w_out):
    c0 = Q_RANK + KV_RANK + QK_ROPE
    u0 = c0 + ATTN_WIDTH
    u1 = u0 + FOURIER_WIDTH
    w1 = jnp.concatenate([_pad_last(w_in[:, :, :c0], 0, LAT_WIDTH - c0), w_in[:, :, u0:u1]],
                         axis=2).astype(BF16)
    wg = jnp.concatenate([w_in[:, :, c0:u0], w_in[:, :, u1:]], axis=2).astype(BF16)

    lead = w_uq.shape[:2]
    wq4 = w_uq.reshape(lead + (N_HEADS, QK_DIM))
    wq = _pad_last(wq4, 0, HEAD_PAD - QK_DIM).reshape(lead + (-1,)).astype(BF16)
    wqp = _pad_last(_swap_halves(wq4[..., QK_NOPE:]), QK_NOPE, HEAD_PAD - QK_DIM)
    wqp = wqp.reshape(lead + (-1,)).astype(BF16)

    wkv4 = w_ukv.reshape(lead + (N_HEADS, QK_NOPE + V_DIM))
    wknT = jnp.swapaxes(wkv4[..., :QK_NOPE].reshape(lead + (-1,)), 1, 2).astype(BF16)
    v5 = wkv4[..., QK_NOPE:].reshape(lead + (N_HEADS // 2, 2, V_DIM))
    wv = jnp.stack([_pad_last(v5[..., 0, :], 0, V_DIM), _pad_last(v5[..., 1, :], V_DIM, 0)], axis=3)
    wv = wv.reshape(lead + (-1,)).astype(BF16)
    within = np.arange(N_HEADS * HEAD_PAD) % (2 * HEAD_PAD)
    vones = jnp.asarray(((within >= V_DIM) & (within < HEAD_PAD + V_DIM))[None, :], F32)

    gq = _pad_last(q_head_g, 0, HEAD_PAD - QK_DIM)[:, None, :]
    gqp = _pad_last(_swap_halves(q_head_g[:, QK_NOPE:]), QK_NOPE, HEAD_PAD - QK_DIM)[:, None, :]
    gk = k_head_g[:, :, None]
    gkp = _swap_halves(k_head_g[:, QK_NOPE:])[:, :, None]
    proj = (norm_g[:, None, :], w1, q_latent_g[:, None, :], kv_latent_g[:, None, :],
            wq, wqp, wknT, wv, vones, gq, gqp, gk, gkp)
    merge = (norm_g[:, None, :], wg, w_attn_proj.astype(BF16), w_fourier_proj.astype(BF16),
             b_merge, w_out.astype(BF16))
    return proj, merge


def kernel(x, norm_g, w_in, q_latent_g, kv_latent_g, w_uq, w_ukv, q_head_g, k_head_g,
           w_attn_proj, w_fourier_proj, b_merge, w_out):
    fc, stage_a, stage_b = _dft_tables()
    rope = _rope_tables()
    proj_params, merge_params = _prepare_params(
        norm_g, w_in, q_latent_g, kv_latent_g, w_uq, w_ukv, q_head_g, k_head_g,
        w_attn_proj, w_fourier_proj, b_merge, w_out)

    x2 = x.reshape(TOKENS, D_MODEL)
    for layer in range(DEPTH):
        q, kt, v, wr, wi = _proj_call(layer, x2, *proj_params, *rope, fc)
        oa = _attn_call(q.reshape(BATCH, SEQ, N_HEADS * HEAD_PAD), kt,
                        v.reshape(BATCH, SEQ, N_HEADS * HEAD_PAD))
        f = _dft_call(wr, wi, stage_a, stage_b)
        x2 = _merge_call(layer, x2, oa.reshape(TOKENS, ATTN_WIDTH),
                         f.reshape(TOKENS, FOURIER_WIDTH), *merge_params)
    return x2.reshape(BATCH, SEQ, D_MODEL)
```

```python
import math

import numpy as np
import jax
import jax.numpy as jnp
from jax import lax
from jax.experimental import pallas as pl
from jax.experimental.pallas import tpu as pltpu

D_MODEL = 1024
BATCH = 4
SEQ = 4096
DEPTH = 4
N_HEADS = 8
QK_NOPE = 64
QK_ROPE = 32
QK_DIM = QK_NOPE + QK_ROPE
V_DIM = 64
HEAD_PAD = 128
ATTN_WIDTH = N_HEADS * V_DIM
Q_RANK = 256
KV_RANK = 256
ROPE_THETA = 10000.0
FOURIER_WIDTH = 512
GROUP_DIM = 128
N_GROUPS = 4
NORM_EPS = 1e-6
RADIX = 64
LAT_WIDTH = 640

F32 = jnp.float32
BF16 = jnp.bfloat16

VMEM_LIMIT = 48 * 1024 * 1024
ATTN_VMEM_LIMIT = 56 * 1024 * 1024

TOKENS = BATCH * SEQ
PROJ_TM = 1024
ATTN_TQ = 256
ATTN_HEADS = 4
MERGE_TM = 512

S2_PER_TILE = PROJ_TM // RADIX
N_HALVES = 2
GROUPS_PER_HALF = N_GROUPS // N_HALVES
HALF_WIDTH = FOURIER_WIDTH // N_HALVES
HALF_LANES = RADIX * HALF_WIDTH
PITCH = RADIX + 8
DFT_S1_CHUNK = 8

Q_SCALE = (QK_DIM ** -0.5) * math.log2(math.e)


def _dot(a, b):
    return jnp.dot(a, b, preferred_element_type=F32)


def _rms(x, axis):
    return lax.rsqrt(jnp.mean(x * x, axis=axis, keepdims=True) + NORM_EPS)


def _proj_body(x_ref, ng_ref, w1_ref, qlg_ref, kvlg_ref, wq_ref, wqp_ref, wknT_ref, wv_ref,
               vones_ref, gq_ref, gqp_ref, gk_ref, gkp_ref, cosq_ref, sinq_ref, cost_ref,
               sint_ref, fc_ref, q_out, kt_out, v_out, wr_out, wi_out, wsr_ref, wsi_ref):
    x = x_ref[...]
    hb = (x * _rms(x, -1) * ng_ref[...]).astype(BF16)
    p = _dot(hb, w1_ref[...])
    cq = p[:, 0:Q_RANK]
    ckv = p[:, Q_RANK:Q_RANK + KV_RANK]
    kpe = p[:, 512:LAT_WIDTH]
    u = p[:, LAT_WIDTH:LAT_WIDTH + FOURIER_WIDTH]

    ch_scale = GROUP_DIM ** -0.5
    fc = fc_ref[...].astype(BF16)
    for g in range(N_GROUPS):
        ug = u[:, g * GROUP_DIM:(g + 1) * GROUP_DIM].astype(BF16)
        wg = _dot(ug, fc) * ch_scale
        for t in range(S2_PER_TILE):
            rows = slice(t * RADIX, (t + 1) * RADIX)
            wsr_ref[g, t * PITCH:t * PITCH + RADIX, :] = wg[rows, :GROUP_DIM]
            wsi_ref[g, t * PITCH:t * PITCH + RADIX, :] = wg[rows, GROUP_DIM:]
    for s1 in range(RADIX):
        for g in range(N_GROUPS):
            half, j = divmod(g, GROUPS_PER_HALF)
            off = half * HALF_LANES + s1 * HALF_WIDTH + j * GROUP_DIM
            pick = pl.ds(s1, S2_PER_TILE, stride=PITCH)
            wr_out[0, :, off:off + GROUP_DIM] = wsr_ref[g, pick, :].astype(BF16)
            wi_out[0, :, off:off + GROUP_DIM] = wsi_ref[g, pick, :].astype(BF16)

    cqn = (cq * _rms(cq, -1) * qlg_ref[...]).astype(BF16)
    ckvn_f = ckv * _rms(ckv, -1) * kvlg_ref[...]
    ckvn = ckvn_f.astype(BF16)

    v_out[...] = (_dot(ckvn, wv_ref[...]) + vones_ref[...]).astype(BF16)

    q_raw = _dot(cqn, wq_ref[...])
    q_par = _dot(cqn, wqp_ref[...])
    gcq = gq_ref[...] * cosq_ref[...]
    gsq = gqp_ref[...] * sinq_ref[...]
    for h in range(N_HEADS):
        sl = slice(h * HEAD_PAD, (h + 1) * HEAD_PAD)
        qh = q_raw[:, sl]
        r = lax.rsqrt(jnp.sum(qh * qh, axis=-1, keepdims=True) * (1.0 / QK_DIM) + NORM_EPS)
        q_out[:, sl] = ((qh * gcq + q_par[:, sl] * gsq) * (r * Q_SCALE)).astype(BF16)

    ckvn_t = ckvn_f.T.astype(BF16)
    kn_t = _dot(wknT_ref[...], ckvn_t)
    kp = kpe.T[0:QK_ROPE, :]
    half = QK_ROPE // 2
    kp_sw = jnp.concatenate([kp[half:], kp[:half]], axis=0)
    gk = gk_ref[...]
    rope = kp * (gk[QK_NOPE:] * cost_ref[...]) + kp_sw * (gkp_ref[...] * sint_ref[...])
    ss_pe = jnp.sum(kp * kp, axis=0, keepdims=True)
    zeros = jnp.zeros((HEAD_PAD - QK_DIM, kp.shape[1]), BF16)
    for h in range(N_HEADS):
        kn = kn_t[h * QK_NOPE:(h + 1) * QK_NOPE, :]
        ss = jnp.sum(kn * kn, axis=0, keepdims=True) + ss_pe
        r = lax.rsqrt(ss * (1.0 / QK_DIM) + NORM_EPS)
        base = h * HEAD_PAD
        kt_out[0, base:base + QK_NOPE, :] = (kn * gk[:QK_NOPE] * r).astype(BF16)
        kt_out[0, base + QK_NOPE:base + QK_DIM, :] = (rope * r).astype(BF16)
        kt_out[0, base + QK_DIM:base + HEAD_PAD, :] = zeros


def _layer_spec(layer, shape):
    return pl.BlockSpec((None,) + shape, lambda *_: (layer,) + (0,) * len(shape))


def _proj_call(layer, x2, ng, w1, qlg, kvlg, wq, wqp, wknT, wv, vones, gq, gqp, gk, gkp,
               cosq, sinq, cost, sint, fc):
    tm = PROJ_TM
    spb = SEQ // tm
    const = lambda shape: pl.BlockSpec(shape, lambda i: (0,) * len(shape))
    per_layer = lambda *shape: _layer_spec(layer, shape)
    in_specs = [
        pl.BlockSpec((tm, D_MODEL), lambda i: (i, 0)),
        per_layer(1, D_MODEL),
        per_layer(D_MODEL, LAT_WIDTH + FOURIER_WIDTH),
        per_layer(1, Q_RANK),
        per_layer(1, KV_RANK),
        per_layer(Q_RANK, N_HEADS * HEAD_PAD),
        per_layer(Q_RANK, N_HEADS * HEAD_PAD),
        per_layer(N_HEADS * QK_NOPE, KV_RANK),
        per_layer(KV_RANK, N_HEADS * HEAD_PAD),
        const((1, N_HEADS * HEAD_PAD)),
        per_layer(1, HEAD_PAD),
        per_layer(1, HEAD_PAD),
        per_layer(QK_DIM, 1),
        per_layer(QK_ROPE, 1),
        pl.BlockSpec((tm, HEAD_PAD), lambda i: (i % spb, 0)),
        pl.BlockSpec((tm, HEAD_PAD), lambda i: (i % spb, 0)),
        pl.BlockSpec((QK_ROPE, tm), lambda i: (0, i % spb)),
        pl.BlockSpec((QK_ROPE, tm), lambda i: (0, i % spb)),
        const((GROUP_DIM, 2 * GROUP_DIM)),
    ]
    out_shape = [
        jax.ShapeDtypeStruct((TOKENS, N_HEADS * HEAD_PAD), BF16),
        jax.ShapeDtypeStruct((BATCH, N_HEADS * HEAD_PAD, SEQ), BF16),
        jax.ShapeDtypeStruct((TOKENS, N_HEADS * HEAD_PAD), BF16),
        jax.ShapeDtypeStruct((BATCH, RADIX, N_HALVES * HALF_LANES), BF16),
        jax.ShapeDtypeStruct((BATCH, RADIX, N_HALVES * HALF_LANES), BF16),
    ]
    dft_in_spec = pl.BlockSpec((1, S2_PER_TILE, N_HALVES * HALF_LANES),
                               lambda i: (i // spb, i % spb, 0))
    out_specs = [
        pl.BlockSpec((tm, N_HEADS * HEAD_PAD), lambda i: (i, 0)),
        pl.BlockSpec((1, N_HEADS * HEAD_PAD, tm), lambda i: (i // spb, 0, i % spb)),
        pl.BlockSpec((tm, N_HEADS * HEAD_PAD), lambda i: (i, 0)),
        dft_in_spec,
        dft_in_spec,
    ]
    regroup = pltpu.VMEM((N_GROUPS, S2_PER_TILE * PITCH, GROUP_DIM), F32)
    return pl.pallas_call(
        _proj_body,
        grid=(TOKENS // tm,),
        in_specs=in_specs,
        out_specs=out_specs,
        out_shape=out_shape,
        scratch_shapes=[regroup, regroup],
        compiler_params=pltpu.CompilerParams(
            dimension_semantics=("arbitrary",), vmem_limit_bytes=VMEM_LIMIT),
        name="proj",
    )(x2, ng, w1, qlg, kvlg, wq, wqp, wknT, wv, vones, gq, gqp, gk, gkp,
      cosq, sinq, cost, sint, fc)


def _attn_body(q_ref, kt_ref, v_ref, o_ref, s_ref, p_ref):
    lanes = lambda u: slice(u * HEAD_PAD, (u + 1) * HEAD_PAD)

    def tile(i, carry):
        rows = pl.ds(pl.multiple_of(i * ATTN_TQ, ATTN_TQ), ATTN_TQ)
        maxes = [None] * ATTN_HEADS
        outs = [None] * ATTN_HEADS
        for t in range(ATTN_HEADS + 2):
            if t < ATTN_HEADS:
                s = _dot(q_ref[0, rows, lanes(t)], kt_ref[0, lanes(t), :])
                s_ref[t % 2] = s
                maxes[t] = jnp.max(s, axis=-1, keepdims=True)
            u = t - 1
            if 0 <= u < ATTN_HEADS:
                p_ref[u % 2] = jnp.exp2(s_ref[u % 2] - maxes[u]).astype(BF16)
            u = t - 2
            if 0 <= u < ATTN_HEADS:
                outs[u] = _dot(p_ref[u % 2], v_ref[0, :, lanes(u)])
        lane = lax.broadcasted_iota(jnp.int32, outs[0].shape, 1)
        for pair in range(ATTN_HEADS // 2):
            even, odd = outs[2 * pair], outs[2 * pair + 1]
            num = jnp.where(lane < V_DIM, even, odd)
            den = jnp.where(lane < V_DIM, pltpu.roll(even, V_DIM, 1), pltpu.roll(odd, V_DIM, 1))
            o_ref[0, rows, pair * HEAD_PAD:(pair + 1) * HEAD_PAD] = (num / den).astype(BF16)
        return carry

    lax.fori_loop(0, SEQ // ATTN_TQ, tile, 0)


def _attn_call(q3, kt, v3):
    tq = ATTN_TQ
    hps = ATTN_HEADS
    return pl.pallas_call(
        _attn_body,
        grid=(BATCH, N_HEADS // hps),
        in_specs=[
            pl.BlockSpec((1, SEQ, hps * HEAD_PAD), lambda b, j: (b, 0, j)),
            pl.BlockSpec((1, hps * HEAD_PAD, SEQ), lambda b, j: (b, j, 0)),
            pl.BlockSpec((1, SEQ, hps * HEAD_PAD), lambda b, j: (b, 0, j)),
        ],
        out_specs=pl.BlockSpec((1, SEQ, hps * V_DIM), lambda b, j: (b, 0, j)),
        out_shape=jax.ShapeDtypeStruct((BATCH, SEQ, ATTN_WIDTH), BF16),
        scratch_shapes=[pltpu.VMEM((2, tq, SEQ), F32), pltpu.VMEM((2, tq, SEQ), BF16)],
        compiler_params=pltpu.CompilerParams(
            dimension_semantics=("arbitrary", "arbitrary"),
            vmem_limit_bytes=ATTN_VMEM_LIMIT),
        name="attn",
    )(q3, kt, v3)


def _dft_body(wr_ref, wi_ref, cs_ref, tb_ref, f_out, xr_ref, xi_ref, y_ref):
    cs = cs_ref[...].astype(BF16)
    for c in range(RADIX // DFT_S1_CHUNK):
        lanes = slice(c * DFT_S1_CHUNK * HALF_WIDTH, (c + 1) * DFT_S1_CHUNK * HALF_WIDTH)
        a = _dot(cs, wr_ref[0, :, lanes])
        b = _dot(cs, wi_ref[0, :, lanes])
        xr = a[:RADIX] + b[RADIX:]
        xi = b[:RADIX] - a[RADIX:]
        for t in range(DFT_S1_CHUNK):
            s1 = c * DFT_S1_CHUNK + t
            for j in range(GROUPS_PER_HALF):
                sub = slice(t * HALF_WIDTH + j * GROUP_DIM, t * HALF_WIDTH + (j + 1) * GROUP_DIM)
                xr_ref[j, s1 * PITCH:s1 * PITCH + RADIX, :] = xr[:, sub]
                xi_ref[j, s1 * PITCH:s1 * PITCH + RADIX, :] = xi[:, sub]
    for k2 in range(RADIX):
        pick = pl.ds(k2, RADIX, stride=PITCH)
        zr = jnp.concatenate([xr_ref[j, pick, :] for j in range(GROUPS_PER_HALF)], axis=1)
        zi = jnp.concatenate([xi_ref[j, pick, :] for j in range(GROUPS_PER_HALF)], axis=1)
        z = jnp.concatenate([zr, zi], axis=0).astype(BF16)
        y = _dot(tb_ref[k2].astype(BF16), z)
        for j in range(GROUPS_PER_HALF):
            y_ref[j, pick, :] = y[:, j * GROUP_DIM:(j + 1) * GROUP_DIM]
    for k1 in range(RADIX):
        rows = slice(k1 * PITCH, k1 * PITCH + RADIX)
        for j in range(GROUPS_PER_HALF):
            f_out[0, k1, :, j * GROUP_DIM:(j + 1) * GROUP_DIM] = y_ref[j, rows, :].astype(BF16)


def _dft_call(wr, wi, cs, tb):
    spec = pl.BlockSpec((1, RADIX, HALF_LANES), lambda b, h: (b, 0, h))
    regroup = pltpu.VMEM((GROUPS_PER_HALF, RADIX * PITCH, GROUP_DIM), F32)
    return pl.pallas_call(
        _dft_body,
        grid=(BATCH, N_HALVES),
        in_specs=[spec, spec,
                  pl.BlockSpec((2 * RADIX, RADIX), lambda b, h: (0, 0)),
                  pl.BlockSpec((RADIX, RADIX, 2 * RADIX), lambda b, h: (0, 0, 0))],
        out_specs=pl.BlockSpec((1, RADIX, RADIX, HALF_WIDTH), lambda b, h: (b, 0, 0, h)),
        out_shape=jax.ShapeDtypeStruct((BATCH, RADIX, RADIX, FOURIER_WIDTH), BF16),
        scratch_shapes=[regroup, regroup, regroup],
        compiler_params=pltpu.CompilerParams(
            dimension_semantics=("arbitrary", "arbitrary"), vmem_limit_bytes=VMEM_LIMIT),
        name="dft",
    )(wr, wi, cs, tb)


def _merge_body(x_ref, oa_ref, f_ref, ng_ref, wg_ref, wa_ref, wf_ref, bm_ref, wo_ref, out_ref):
    x = x_ref[...]
    hb = (x * _rms(x, -1) * ng_ref[...]).astype(BF16)
    gates = _dot(hb, wg_ref[...])
    z_a = gates[:, 0:ATTN_WIDTH]
    z_f = gates[:, ATTN_WIDTH:ATTN_WIDTH + FOURIER_WIDTH]
    g_a = gates[:, 1024:1024 + D_MODEL]
    g_f = gates[:, 1024 + D_MODEL:]
    ya = _dot((oa_ref[...].astype(F32) * (z_a * jax.nn.sigmoid(z_a))).astype(BF16), wa_ref[...])
    yf = _dot((f_ref[...].astype(F32) * (z_f * jax.nn.sigmoid(z_f))).astype(BF16), wf_ref[...])
    bm = bm_ref[...]
    m = jax.nn.sigmoid(g_a + bm[0:1]) * ya + jax.nn.sigmoid(g_f + bm[1:2]) * yf
    out_ref[...] = x + _dot(m.astype(BF16), wo_ref[...])


def _merge_call(layer, x2, oa, f, ng, wg, wa, wf, bm, wo):
    tm = MERGE_TM
    per_layer = lambda *shape: _layer_spec(layer, shape)
    return pl.pallas_call(
        _merge_body,
        grid=(TOKENS // tm,),
        in_specs=[
            pl.BlockSpec((tm, D_MODEL), lambda i: (i, 0)),
            pl.BlockSpec((tm, ATTN_WIDTH), lambda i: (i, 0)),
            pl.BlockSpec((tm, FOURIER_WIDTH), lambda i: (i, 0)),
            per_layer(1, D_MODEL),
            per_layer(D_MODEL, 3 * D_MODEL),
            per_layer(ATTN_WIDTH, D_MODEL),
            per_layer(FOURIER_WIDTH, D_MODEL),
            per_layer(2, D_MODEL),
            per_layer(D_MODEL, D_MODEL),
        ],
        out_specs=pl.BlockSpec((tm, D_MODEL), lambda i: (i, 0)),
        out_shape=jax.ShapeDtypeStruct((TOKENS, D_MODEL), F32),
        compiler_params=pltpu.CompilerParams(
            dimension_semantics=("arbitrary",), vmem_limit_bytes=VMEM_LIMIT),
        name="merge",
    )(x2, oa, f, ng, wg, wa, wf, bm, wo)


def _dft_tables():
    def cs(num, den):
        ang = 2.0 * np.pi * (num % den).astype(np.float64) / den
        return np.cos(ang), np.sin(ang)

    c = np.arange(GROUP_DIM)
    cc, sc = cs(np.outer(c, c), GROUP_DIM)
    fc = np.concatenate([cc, -sc], axis=1)

    k = np.arange(RADIX)
    ca, sa = cs(np.outer(k, k), RADIX)
    stage_a = np.concatenate([ca, sa], axis=0) / 8.0

    k2 = k[:, None, None]
    k1 = k[None, :, None]
    s1 = k[None, None, :]
    cb, sb = cs(s1 * (RADIX * k1 + k2), SEQ)
    stage_b = np.concatenate([cb, sb], axis=2) / 8.0
    return jnp.asarray(fc, F32), jnp.asarray(stage_a, F32), jnp.asarray(stage_b, F32)


def _rope_tables():
    half = QK_ROPE // 2
    inv_freq = ROPE_THETA ** (-np.arange(half, dtype=np.float64) / half)
    ang = np.arange(SEQ, dtype=np.float64)[:, None] * inv_freq[None, :]
    cos, sin = np.cos(ang), np.sin(ang)
    cos32 = np.concatenate([cos, cos], axis=1)
    sin32 = np.concatenate([-sin, sin], axis=1)
    pad = np.zeros((SEQ, HEAD_PAD - QK_DIM))
    cosq = np.concatenate([np.ones((SEQ, QK_NOPE)), cos32, pad], axis=1)
    sinq = np.concatenate([np.zeros((SEQ, QK_NOPE)), sin32, pad], axis=1)
    return tuple(jnp.asarray(t, F32) for t in (cosq, sinq, cos32.T, sin32.T))


def _swap_halves(a):
    half = a.shape[-1] // 2
    return jnp.concatenate([a[..., half:], a[..., :half]], axis=-1)


def _pad_last(a, before, after):
    return jnp.pad(a, [(0, 0)] * (a.ndim - 1) + [(before, after)])


def _prepare_params(norm_g, w_in, q_latent_g, kv_latent_g, w_uq, w_ukv, q_head_g, k_head_g,
                    w_attn_proj, w_fourier_proj, b_merge, w_out):
    c0 = Q_RANK + KV_RANK + QK_ROPE
    u0 = c0 + ATTN_WIDTH
    u1 = u0 + FOURIER_WIDTH
    w1 = jnp.concatenate([_pad_last(w_in[:, :, :c0], 0, LAT_WIDTH - c0), w_in[:, :, u0:u1]],
                         axis=2).astype(BF16)
    wg = jnp.concatenate([w_in[:, :, c0:u0], w_in[:, :, u1:]], axis=2).astype(BF16)

    lead = w_uq.shape[:2]
    wq4 = w_uq.reshape(lead + (N_HEADS, QK_DIM))
    wq = _pad_last(wq4, 0, HEAD_PAD - QK_DIM).reshape(lead + (-1,)).astype(BF16)
    wqp = _pad_last(_swap_halves(wq4[..., QK_NOPE:]), QK_NOPE, HEAD_PAD - QK_DIM)
    wqp = wqp.reshape(lead + (-1,)).astype(BF16)

    wkv4 = w_ukv.reshape(lead + (N_HEADS, QK_NOPE + V_DIM))
    wknT = jnp.swapaxes(wkv4[..., :QK_NOPE].reshape(lead + (-1,)), 1, 2).astype(BF16)
    v5 = wkv4[..., QK_NOPE:].reshape(lead + (N_HEADS // 2, 2, V_DIM))
    wv = jnp.stack([_pad_last(v5[..., 0, :], 0, V_DIM), _pad_last(v5[..., 1, :], V_DIM, 0)], axis=3)
    wv = wv.reshape(lead + (-1,)).astype(BF16)
    within = np.arange(N_HEADS * HEAD_PAD) % (2 * HEAD_PAD)
    vones = jnp.asarray(((within >= V_DIM) & (within < HEAD_PAD + V_DIM))[None, :], F32)

    gq = _pad_last(q_head_g, 0, HEAD_PAD - QK_DIM)[:, None, :]
    gqp = _pad_last(_swap_halves(q_head_g[:, QK_NOPE:]), QK_NOPE, HEAD_PAD - QK_DIM)[:, None, :]
    gk = k_head_g[:, :, None]
    gkp = _swap_halves(k_head_g[:, QK_NOPE:])[:, :, None]
    proj = (norm_g[:, None, :], w1, q_latent_g[:, None, :], kv_latent_g[:, None, :],
            wq, wqp, wknT, wv, vones, gq, gqp, gk, gkp)
    merge = (norm_g[:, None, :], wg, w_attn_proj.astype(BF16), w_fourier_proj.astype(BF16),
             b_merge, w_out.astype(BF16))
    return proj, merge


def kernel(x, norm_g, w_in, q_latent_g, kv_latent_g, w_uq, w_ukv, q_head_g, k_head_g,
           w_attn_proj, w_fourier_proj, b_merge, w_out):
    fc, stage_a, stage_b = _dft_tables()
    rope = _rope_tables()
    proj_params, merge_params = _prepare_params(
        norm_g, w_in, q_latent_g, kv_latent_g, w_uq, w_ukv, q_head_g, k_head_g,
        w_attn_proj, w_fourier_proj, b_merge, w_out)

    x2 = x.reshape(TOKENS, D_MODEL)
    for layer in range(DEPTH):
        q, kt, v, wr, wi = _proj_call(layer, x2, *proj_params, *rope, fc)
        oa = _attn_call(q.reshape(BATCH, SEQ, N_HEADS * HEAD_PAD), kt,
                        v.reshape(BATCH, SEQ, N_HEADS * HEAD_PAD))
        f = _dft_call(wr, wi, stage_a, stage_b)
        x2 = _merge_call(layer, x2, oa.reshape(TOKENS, ATTN_WIDTH),
                         f.reshape(TOKENS, FOURIER_WIDTH), *merge_params)
    return x2.reshape(BATCH, SEQ, D_MODEL)
```

```python
import math

import numpy as np
import jax
import jax.numpy as jnp
from jax import lax
from jax.experimental import pallas as pl
from jax.experimental.pallas import tpu as pltpu

D_MODEL = 1024
BATCH = 4
SEQ = 4096
DEPTH = 4
N_HEADS = 8
QK_NOPE = 64
QK_ROPE = 32
QK_DIM = QK_NOPE + QK_ROPE
V_DIM = 64
HEAD_PAD = 128
ATTN_WIDTH = N_HEADS * V_DIM
Q_RANK = 256
KV_RANK = 256
ROPE_THETA = 10000.0
FOURIER_WIDTH = 512
GROUP_DIM = 128
N_GROUPS = 4
NORM_EPS = 1e-6
RADIX = 64
LAT_WIDTH = 640

F32 = jnp.float32
BF16 = jnp.bfloat16

VMEM_LIMIT = 48 * 1024 * 1024
ATTN_VMEM_LIMIT = 56 * 1024 * 1024

TOKENS = BATCH * SEQ
PROJ_TM = 1024
PROJ_SUB = 512
ATTN_TQ = 256
ATTN_HEADS = 4
MERGE_TM = 1024
MERGE_SUB = 512

S2_PER_TILE = PROJ_TM // RADIX
N_HALVES = 2
GROUPS_PER_HALF = N_GROUPS // N_HALVES
HALF_WIDTH = FOURIER_WIDTH // N_HALVES
HALF_LANES = RADIX * HALF_WIDTH
PITCH = RADIX + 8
DFT_S1_CHUNK = 8

Q_SCALE = (QK_DIM ** -0.5) * math.log2(math.e)


def _dot(a, b):
    return jnp.dot(a, b, preferred_element_type=F32)


def _rms(x, axis):
    return lax.rsqrt(jnp.mean(x * x, axis=axis, keepdims=True) + NORM_EPS)


def _proj_body(x_ref, ng_ref, w1_ref, qlg_ref, kvlg_ref, wq_ref, wqp_ref, wknT_ref, wv_ref,
               vones_ref, gq_ref, gqp_ref, gk_ref, gkp_ref, cosq_ref, sinq_ref, cost_ref,
               sint_ref, fc_ref, q_out, kt_out, v_out, wr_out, wi_out, wsr_ref, wsi_ref):
    for sub in range(PROJ_TM // PROJ_SUB):
        _proj_rows(sub, x_ref, ng_ref, w1_ref, qlg_ref, kvlg_ref, wq_ref, wqp_ref, wknT_ref,
                   wv_ref, vones_ref, gq_ref, gqp_ref, gk_ref, gkp_ref, cosq_ref, sinq_ref,
                   cost_ref, sint_ref, fc_ref, q_out, kt_out, v_out, wsr_ref, wsi_ref)
    for s1 in range(RADIX):
        for g in range(N_GROUPS):
            half, j = divmod(g, GROUPS_PER_HALF)
            off = half * HALF_LANES + s1 * HALF_WIDTH + j * GROUP_DIM
            pick = pl.ds(s1, S2_PER_TILE, stride=PITCH)
            wr_out[0, :, off:off + GROUP_DIM] = wsr_ref[g, pick, :].astype(BF16)
            wi_out[0, :, off:off + GROUP_DIM] = wsi_ref[g, pick, :].astype(BF16)


def _proj_rows(sub, x_ref, ng_ref, w1_ref, qlg_ref, kvlg_ref, wq_ref, wqp_ref, wknT_ref, wv_ref,
               vones_ref, gq_ref, gqp_ref, gk_ref, gkp_ref, cosq_ref, sinq_ref, cost_ref,
               sint_ref, fc_ref, q_out, kt_out, v_out, wsr_ref, wsi_ref):
    rows = slice(sub * PROJ_SUB, (sub + 1) * PROJ_SUB)
    x = x_ref[rows, :]
    hb = (x * _rms(x, -1) * ng_ref[...]).astype(BF16)
    p = _dot(hb, w1_ref[...])
    cq = p[:, 0:Q_RANK]
    ckv = p[:, Q_RANK:Q_RANK + KV_RANK]
    kpe = p[:, 512:LAT_WIDTH]
    u = p[:, LAT_WIDTH:LAT_WIDTH + FOURIER_WIDTH]

    ch_scale = GROUP_DIM ** -0.5
    fc = fc_ref[...].astype(BF16)
    s2_per_sub = PROJ_SUB // RADIX
    for g in range(N_GROUPS):
        ug = u[:, g * GROUP_DIM:(g + 1) * GROUP_DIM].astype(BF16)
        wg = _dot(ug, fc) * ch_scale
        for t in range(s2_per_sub):
            slab = slice(t * RADIX, (t + 1) * RADIX)
            slot = (sub * s2_per_sub + t) * PITCH
            wsr_ref[g, slot:slot + RADIX, :] = wg[slab, :GROUP_DIM]
            wsi_ref[g, slot:slot + RADIX, :] = wg[slab, GROUP_DIM:]

    cqn = (cq * _rms(cq, -1) * qlg_ref[...]).astype(BF16)
    ckvn_f = ckv * _rms(ckv, -1) * kvlg_ref[...]
    ckvn = ckvn_f.astype(BF16)

    v_out[rows, :] = (_dot(ckvn, wv_ref[...]) + vones_ref[...]).astype(BF16)

    q_raw = _dot(cqn, wq_ref[...])
    q_par = _dot(cqn, wqp_ref[...])
    gcq = gq_ref[...] * cosq_ref[rows, :]
    gsq = gqp_ref[...] * sinq_ref[rows, :]
    for h in range(N_HEADS):
        sl = slice(h * HEAD_PAD, (h + 1) * HEAD_PAD)
        qh = q_raw[:, sl]
        r = lax.rsqrt(jnp.sum(qh * qh, axis=-1, keepdims=True) * (1.0 / QK_DIM) + NORM_EPS)
        q_out[rows, sl] = ((qh * gcq + q_par[:, sl] * gsq) * (r * Q_SCALE)).astype(BF16)

    ckvn_t = ckvn_f.T.astype(BF16)
    kn_t = _dot(wknT_ref[...], ckvn_t)
    kp = kpe.T[0:QK_ROPE, :]
    kp_sw = jnp.concatenate([kp[QK_ROPE // 2:], kp[:QK_ROPE // 2]], axis=0)
    gk = gk_ref[...]
    rope = (kp * (gk[QK_NOPE:] * cost_ref[:, rows])
            + kp_sw * (gkp_ref[...] * sint_ref[:, rows]))
    ss_pe = jnp.sum(kp * kp, axis=0, keepdims=True)
    zeros = jnp.zeros((HEAD_PAD - QK_DIM, PROJ_SUB), BF16)
    for h in range(N_HEADS):
        kn = kn_t[h * QK_NOPE:(h + 1) * QK_NOPE, :]
        ss = jnp.sum(kn * kn, axis=0, keepdims=True) + ss_pe
        r = lax.rsqrt(ss * (1.0 / QK_DIM) + NORM_EPS)
        base = h * HEAD_PAD
        kt_out[0, base:base + QK_NOPE, rows] = (kn * gk[:QK_NOPE] * r).astype(BF16)
        kt_out[0, base + QK_NOPE:base + QK_DIM, rows] = (rope * r).astype(BF16)
        kt_out[0, base + QK_DIM:base + HEAD_PAD, rows] = zeros


def _layer_spec(layer, shape):
    return pl.BlockSpec((None,) + shape, lambda *_: (layer,) + (0,) * len(shape))


def _proj_call(layer, x2, ng, w1, qlg, kvlg, wq, wqp, wknT, wv, vones, gq, gqp, gk, gkp,
               cosq, sinq, cost, sint, fc):
    tm = PROJ_TM
    spb = SEQ // tm
    const = lambda shape: pl.BlockSpec(shape, lambda i: (0,) * len(shape))
    per_layer = lambda *shape: _layer_spec(layer, shape)
    in_specs = [
        pl.BlockSpec((tm, D_MODEL), lambda i: (i, 0)),
        per_layer(1, D_MODEL),
        per_layer(D_MODEL, LAT_WIDTH + FOURIER_WIDTH),
        per_layer(1, Q_RANK),
        per_layer(1, KV_RANK),
        per_layer(Q_RANK, N_HEADS * HEAD_PAD),
        per_layer(Q_RANK, N_HEADS * HEAD_PAD),
        per_layer(N_HEADS * QK_NOPE, KV_RANK),
        per_layer(KV_RANK, N_HEADS * HEAD_PAD),
        const((1, N_HEADS * HEAD_PAD)),
        per_layer(1, HEAD_PAD),
        per_layer(1, HEAD_PAD),
        per_layer(QK_DIM, 1),
        per_layer(QK_ROPE, 1),
        pl.BlockSpec((tm, HEAD_PAD), lambda i: (i % spb, 0)),
        pl.BlockSpec((tm, HEAD_PAD), lambda i: (i % spb, 0)),
        pl.BlockSpec((QK_ROPE, tm), lambda i: (0, i % spb)),
        pl.BlockSpec((QK_ROPE, tm), lambda i: (0, i % spb)),
        const((GROUP_DIM, 2 * GROUP_DIM)),
    ]
    out_shape = [
        jax.ShapeDtypeStruct((TOKENS, N_HEADS * HEAD_PAD), BF16),
        jax.ShapeDtypeStruct((BATCH, N_HEADS * HEAD_PAD, SEQ), BF16),
        jax.ShapeDtypeStruct((TOKENS, N_HEADS * HEAD_PAD), BF16),
        jax.ShapeDtypeStruct((BATCH, RADIX, N_HALVES * HALF_LANES), BF16),
        jax.ShapeDtypeStruct((BATCH, RADIX, N_HALVES * HALF_LANES), BF16),
    ]
    dft_in_spec = pl.BlockSpec((1, S2_PER_TILE, N_HALVES * HALF_LANES),
                               lambda i: (i // spb, i % spb, 0))
    out_specs = [
        pl.BlockSpec((tm, N_HEADS * HEAD_PAD), lambda i: (i, 0)),
        pl.BlockSpec((1, N_HEADS * HEAD_PAD, tm), lambda i: (i // spb, 0, i % spb)),
        pl.BlockSpec((tm, N_HEADS * HEAD_PAD), lambda i: (i, 0)),
        dft_in_spec,
        dft_in_spec,
    ]
    regroup = pltpu.VMEM((N_GROUPS, S2_PER_TILE * PITCH, GROUP_DIM), F32)
    return pl.pallas_call(
        _proj_body,
        grid=(TOKENS // tm,),
        in_specs=in_specs,
        out_specs=out_specs,
        out_shape=out_shape,
        scratch_shapes=[regroup, regroup],
        compiler_params=pltpu.CompilerParams(
            dimension_semantics=("arbitrary",), vmem_limit_bytes=VMEM_LIMIT),
        name="proj",
    )(x2, ng, w1, qlg, kvlg, wq, wqp, wknT, wv, vones, gq, gqp, gk, gkp,
      cosq, sinq, cost, sint, fc)


def _attn_body(q_ref, kt_ref, v_ref, o_ref, s_ref, p_ref):
    lanes = lambda u: slice(u * HEAD_PAD, (u + 1) * HEAD_PAD)

    def tile(i, carry):
        rows = pl.ds(pl.multiple_of(i * ATTN_TQ, ATTN_TQ), ATTN_TQ)
        maxes = [None] * ATTN_HEADS
        outs = [None] * ATTN_HEADS
        for t in range(ATTN_HEADS + 2):
            if t < ATTN_HEADS:
                s = _dot(q_ref[0, rows, lanes(t)], kt_ref[0, lanes(t), :])
                s_ref[t % 2] = s
                maxes[t] = jnp.max(s, axis=-1, keepdims=True)
            u = t - 1
            if 0 <= u < ATTN_HEADS:
                p_ref[u % 2] = jnp.exp2(s_ref[u % 2] - maxes[u]).astype(BF16)
            u = t - 2
            if 0 <= u < ATTN_HEADS:
                outs[u] = _dot(p_ref[u % 2], v_ref[0, :, lanes(u)])
        lane = lax.broadcasted_iota(jnp.int32, outs[0].shape, 1)
        for pair in range(ATTN_HEADS // 2):
            even, odd = outs[2 * pair], outs[2 * pair + 1]
            num = jnp.where(lane < V_DIM, even, odd)
            den = jnp.where(lane < V_DIM, pltpu.roll(even, V_DIM, 1), pltpu.roll(odd, V_DIM, 1))
            o_ref[0, rows, pair * HEAD_PAD:(pair + 1) * HEAD_PAD] = (num / den).astype(BF16)
        return carry

    lax.fori_loop(0, SEQ // ATTN_TQ, tile, 0)


def _attn_call(q3, kt, v3):
    tq = ATTN_TQ
    hps = ATTN_HEADS
    return pl.pallas_call(
        _attn_body,
        grid=(BATCH, N_HEADS // hps),
        in_specs=[
            pl.BlockSpec((1, SEQ, hps * HEAD_PAD), lambda b, j: (b, 0, j)),
            pl.BlockSpec((1, hps * HEAD_PAD, SEQ), lambda b, j: (b, j, 0)),
            pl.BlockSpec((1, SEQ, hps * HEAD_PAD), lambda b, j: (b, 0, j)),
        ],
        out_specs=pl.BlockSpec((1, SEQ, hps * V_DIM), lambda b, j: (b, 0, j)),
        out_shape=jax.ShapeDtypeStruct((BATCH, SEQ, ATTN_WIDTH), BF16),
        scratch_shapes=[pltpu.VMEM((2, tq, SEQ), F32), pltpu.VMEM((2, tq, SEQ), BF16)],
        compiler_params=pltpu.CompilerParams(
            dimension_semantics=("arbitrary", "arbitrary"),
            vmem_limit_bytes=ATTN_VMEM_LIMIT),
        name="attn",
    )(q3, kt, v3)


def _dft_body(wr_ref, wi_ref, cs_ref, tb_ref, f_out, xr_ref, xi_ref, y_ref):
    cs = cs_ref[...].astype(BF16)
    for c in range(RADIX // DFT_S1_CHUNK):
        lanes = slice(c * DFT_S1_CHUNK * HALF_WIDTH, (c + 1) * DFT_S1_CHUNK * HALF_WIDTH)
        w = jnp.concatenate([wr_ref[0, :, lanes], wi_ref[0, :, lanes]], axis=0)
        x = _dot(cs, w)
        xr = x[:RADIX]
        xi = x[RADIX:]
        for t in range(DFT_S1_CHUNK):
            s1 = c * DFT_S1_CHUNK + t
            for j in range(GROUPS_PER_HALF):
                sub = slice(t * HALF_WIDTH + j * GROUP_DIM, t * HALF_WIDTH + (j + 1) * GROUP_DIM)
                xr_ref[j, s1 * PITCH:s1 * PITCH + RADIX, :] = xr[:, sub]
                xi_ref[j, s1 * PITCH:s1 * PITCH + RADIX, :] = xi[:, sub]
    for k2 in range(RADIX):
        pick = pl.ds(k2, RADIX, stride=PITCH)
        zr = jnp.concatenate([xr_ref[j, pick, :] for j in range(GROUPS_PER_HALF)], axis=1)
        zi = jnp.concatenate([xi_ref[j, pick, :] for j in range(GROUPS_PER_HALF)], axis=1)
        z = jnp.concatenate([zr, zi], axis=0).astype(BF16)
        y = _dot(tb_ref[k2].astype(BF16), z)
        for j in range(GROUPS_PER_HALF):
            y_ref[j, pick, :] = y[:, j * GROUP_DIM:(j + 1) * GROUP_DIM]
    for k1 in range(RADIX):
        rows = slice(k1 * PITCH, k1 * PITCH + RADIX)
        for j in range(GROUPS_PER_HALF):
            f_out[0, k1, :, j * GROUP_DIM:(j + 1) * GROUP_DIM] = y_ref[j, rows, :].astype(BF16)


def _dft_call(wr, wi, cs, tb):
    spec = pl.BlockSpec((1, RADIX, HALF_LANES), lambda b, h: (b, 0, h))
    regroup = pltpu.VMEM((GROUPS_PER_HALF, RADIX * PITCH, GROUP_DIM), F32)
    return pl.pallas_call(
        _dft_body,
        grid=(BATCH, N_HALVES),
        in_specs=[spec, spec,
                  pl.BlockSpec((2 * RADIX, 2 * RADIX), lambda b, h: (0, 0)),
                  pl.BlockSpec((RADIX, RADIX, 2 * RADIX), lambda b, h: (0, 0, 0))],
        out_specs=pl.BlockSpec((1, RADIX, RADIX, HALF_WIDTH), lambda b, h: (b, 0, 0, h)),
        out_shape=jax.ShapeDtypeStruct((BATCH, RADIX, RADIX, FOURIER_WIDTH), BF16),
        scratch_shapes=[regroup, regroup, regroup],
        compiler_params=pltpu.CompilerParams(
            dimension_semantics=("arbitrary", "arbitrary"), vmem_limit_bytes=VMEM_LIMIT),
        name="dft",
    )(wr, wi, cs, tb)


def _merge_body(x_ref, oa_ref, f_ref, ng_ref, wg_ref, wa_ref, wf_ref, bm_ref, wo_ref, out_ref):
    def sub_tile(i, carry):
        rows = pl.ds(pl.multiple_of(i * MERGE_SUB, MERGE_SUB), MERGE_SUB)
        x = x_ref[rows, :]
        hb = (x * _rms(x, -1) * ng_ref[...]).astype(BF16)
        gates = _dot(hb, wg_ref[...])
        z_a = gates[:, 0:ATTN_WIDTH]
        z_f = gates[:, ATTN_WIDTH:ATTN_WIDTH + FOURIER_WIDTH]
        g_a = gates[:, 1024:1024 + D_MODEL]
        g_f = gates[:, 1024 + D_MODEL:]
        oa = oa_ref[rows, :].astype(F32)
        f = f_ref[rows, :].astype(F32)
        ya = _dot((oa * (z_a * jax.nn.sigmoid(z_a))).astype(BF16), wa_ref[...])
        yf = _dot((f * (z_f * jax.nn.sigmoid(z_f))).astype(BF16), wf_ref[...])
        bm = bm_ref[...]
        m = jax.nn.sigmoid(g_a + bm[0:1]) * ya + jax.nn.sigmoid(g_f + bm[1:2]) * yf
        out_ref[rows, :] = x + _dot(m.astype(BF16), wo_ref[...])
        return carry

    lax.fori_loop(0, MERGE_TM // MERGE_SUB, sub_tile, 0)


def _merge_call(layer, x2, oa, f, ng, wg, wa, wf, bm, wo):
    tm = MERGE_TM
    per_layer = lambda *shape: pl.BlockSpec(
        (None,) + shape, lambda i: (layer,) + (0,) * len(shape), pipeline_mode=pl.Buffered(1))
    return pl.pallas_call(
        _merge_body,
        grid=(TOKENS // tm,),
        in_specs=[
            pl.BlockSpec((tm, D_MODEL), lambda i: (i, 0)),
            pl.BlockSpec((tm, ATTN_WIDTH), lambda i: (i, 0)),
            pl.BlockSpec((tm, FOURIER_WIDTH), lambda i: (i, 0)),
            per_layer(1, D_MODEL),
            per_layer(D_MODEL, 3 * D_MODEL),
            per_layer(ATTN_WIDTH, D_MODEL),
            per_layer(FOURIER_WIDTH, D_MODEL),
            per_layer(2, D_MODEL),
            per_layer(D_MODEL, D_MODEL),
        ],
        out_specs=pl.BlockSpec((tm, D_MODEL), lambda i: (i, 0)),
        out_shape=jax.ShapeDtypeStruct((TOKENS, D_MODEL), F32),
        compiler_params=pltpu.CompilerParams(
            dimension_semantics=("arbitrary",), vmem_limit_bytes=VMEM_LIMIT),
        name="merge",
    )(x2, oa, f, ng, wg, wa, wf, bm, wo)


def _dft_tables():
    def cs(num, den):
        ang = 2.0 * np.pi * (num % den).astype(np.float64) / den
        return np.cos(ang), np.sin(ang)

    c = np.arange(GROUP_DIM)
    cc, sc = cs(np.outer(c, c), GROUP_DIM)
    fc = np.concatenate([cc, -sc], axis=1)

    k = np.arange(RADIX)
    ca, sa = cs(np.outer(k, k), RADIX)
    stage_a = np.block([[ca, sa], [-sa, ca]]) / 8.0

    k2 = k[:, None, None]
    k1 = k[None, :, None]
    s1 = k[None, None, :]
    cb, sb = cs(s1 * (RADIX * k1 + k2), SEQ)
    stage_b = np.concatenate([cb, sb], axis=2) / 8.0
    return jnp.asarray(fc, F32), jnp.asarray(stage_a, F32), jnp.asarray(stage_b, F32)


def _rope_tables():
    half = QK_ROPE // 2
    inv_freq = ROPE_THETA ** (-np.arange(half, dtype=np.float64) / half)
    ang = np.arange(SEQ, dtype=np.float64)[:, None] * inv_freq[None, :]
    cos, sin = np.cos(ang), np.sin(ang)
    cos32 = np.concatenate([cos, cos], axis=1)
    sin32 = np.concatenate([-sin, sin], axis=1)
    pad = np.zeros((SEQ, HEAD_PAD - QK_DIM))
    cosq = np.concatenate([np.ones((SEQ, QK_NOPE)), cos32, pad], axis=1)
    sinq = np.concatenate([np.zeros((SEQ, QK_NOPE)), sin32, pad], axis=1)
    return tuple(jnp.asarray(t, F32) for t in (cosq, sinq, cos32.T, sin32.T))


def _swap_halves(a):
    half = a.shape[-1] // 2
    return jnp.concatenate([a[..., half:], a[..., :half]], axis=-1)


def _pad_last(a, before, after):
    return jnp.pad(a, [(0, 0)] * (a.ndim - 1) + [(before, after)])


def _prepare_params(norm_g, w_in, q_latent_g, kv_latent_g, w_uq, w_ukv, q_head_g, k_head_g,
                    w_attn_proj, w_fourier_proj, b_merge, w_out):
    c0 = Q_RANK + KV_RANK + QK_ROPE
    u0 = c0 + ATTN_WIDTH
    u1 = u0 + FOURIER_WIDTH
    w1 = jnp.concatenate([_pad_last(w_in[:, :, :c0], 0, LAT_WIDTH - c0), w_in[:, :, u0:u1]],
                         axis=2).astype(BF16)
    wg = jnp.concatenate([w_in[:, :, c0:u0], w_in[:, :, u1:]], axis=2).astype(BF16)

    lead = w_uq.shape[:2]
    wq4 = w_uq.reshape(lead + (N_HEADS, QK_DIM))
    wq = _pad_last(wq4, 0, HEAD_PAD - QK_DIM).reshape(lead + (-1,)).astype(BF16)
    wqp = _pad_last(_swap_halves(wq4[..., QK_NOPE:]), QK_NOPE, HEAD_PAD - QK_DIM)
    wqp = wqp.reshape(lead + (-1,)).astype(BF16)

    wkv4 = w_ukv.reshape(lead + (N_HEADS, QK_NOPE + V_DIM))
    wknT = jnp.swapaxes(wkv4[..., :QK_NOPE].reshape(lead + (-1,)), 1, 2).astype(BF16)
    v5 = wkv4[..., QK_NOPE:].reshape(lead + (N_HEADS // 2, 2, V_DIM))
    wv = jnp.stack([_pad_last(v5[..., 0, :], 0, V_DIM), _pad_last(v5[..., 1, :], V_DIM, 0)], axis=3)
    wv = wv.reshape(lead + (-1,)).astype(BF16)
    within = np.arange(N_HEADS * HEAD_PAD) % (2 * HEAD_PAD)
    vones = jnp.asarray(((within >= V_DIM) & (within < HEAD_PAD + V_DIM))[None, :], F32)

    gq = _pad_last(q_head_g, 0, HEAD_PAD - QK_DIM)[:, None, :]
    gqp = _pad_last(_swap_halves(q_head_g[:, QK_NOPE:]), QK_NOPE, HEAD_PAD - QK_DIM)[:, None, :]
    gk = k_head_g[:, :, None]
    gkp = _swap_halves(k_head_g[:, QK_NOPE:])[:, :, None]
    proj = (norm_g[:, None, :], w1, q_latent_g[:, None, :], kv_latent_g[:, None, :],
            wq, wqp, wknT, wv, vones, gq, gqp, gk, gkp)
    merge = (norm_g[:, None, :], wg, w_attn_proj.astype(BF16), w_fourier_proj.astype(BF16),
             b_merge, w_out.astype(BF16))
    return proj, merge


def kernel(x, norm_g, w_in, q_latent_g, kv_latent_g, w_uq, w_ukv, q_head_g, k_head_g,
           w_attn_proj, w_fourier_proj, b_merge, w_out):
    fc, stage_a, stage_b = _dft_tables()
    rope = _rope_tables()
    proj_params, merge_params = _prepare_params(
        norm_g, w_in, q_latent_g, kv_latent_g, w_uq, w_ukv, q_head_g, k_head_g,
        w_attn_proj, w_fourier_proj, b_merge, w_out)

    x2 = x.reshape(TOKENS, D_MODEL)
    for layer in range(DEPTH):
        q, kt, v, wr, wi = _proj_call(layer, x2, *proj_params, *rope, fc)
        oa = _attn_call(q.reshape(BATCH, SEQ, N_HEADS * HEAD_PAD), kt,
                        v.reshape(BATCH, SEQ, N_HEADS * HEAD_PAD))
        f = _dft_call(wr, wi, stage_a, stage_b)
        x2 = _merge_call(layer, x2, oa.reshape(TOKENS, ATTN_WIDTH),
                         f.reshape(TOKENS, FOURIER_WIDTH), *merge_params)
    return x2.reshape(BATCH, SEQ, D_MODEL)
```

```python
import math

import numpy as np
import jax
import jax.numpy as jnp
from jax import lax
from jax.experimental import pallas as pl
from jax.experimental.pallas import tpu as pltpu

D_MODEL = 1024
BATCH = 4
SEQ = 4096
DEPTH = 4
N_HEADS = 8
QK_NOPE = 64
QK_ROPE = 32
QK_DIM = QK_NOPE + QK_ROPE
V_DIM = 64
HEAD_PAD = 128
ATTN_WIDTH = N_HEADS * V_DIM
Q_RANK = 256
KV_RANK = 256
ROPE_THETA = 10000.0
FOURIER_WIDTH = 512
GROUP_DIM = 128
N_GROUPS = 4
NORM_EPS = 1e-6
RADIX = 64
LAT_WIDTH = 640

F32 = jnp.float32
BF16 = jnp.bfloat16

VMEM_LIMIT = 48 * 1024 * 1024
ATTN_VMEM_LIMIT = 56 * 1024 * 1024

TOKENS = BATCH * SEQ
PROJ_TM = 1024
PROJ_SUB = 512
ATTN_TQ = 256
ATTN_HEADS = 4
MAX_SAFE_SCORE_BOUND = 40.0
MERGE_TM = 1024
MERGE_SUB = 512

S2_PER_TILE = PROJ_TM // RADIX
N_HALVES = 2
GROUPS_PER_HALF = N_GROUPS // N_HALVES
HALF_WIDTH = FOURIER_WIDTH // N_HALVES
HALF_LANES = RADIX * HALF_WIDTH
PITCH = RADIX + 8
DFT_S1_CHUNK = 8

Q_SCALE = (QK_DIM ** -0.5) * math.log2(math.e)


def _dot(a, b):
    return jnp.dot(a, b, preferred_element_type=F32)


def _rms(x, axis):
    return lax.rsqrt(jnp.mean(x * x, axis=axis, keepdims=True) + NORM_EPS)


def _proj_body(x_ref, ng_ref, w1_ref, qlg_ref, kvlg_ref, wq_ref, wqp_ref, wknT_ref, wv_ref,
               qshift_ref, gq_ref, gqp_ref, gk_ref, gkp_ref, cosq_ref, sinq_ref, cost_ref,
               sint_ref, fc_ref, q_out, kt_out, v_out, wr_out, wi_out, wsr_ref, wsi_ref):
    for sub in range(PROJ_TM // PROJ_SUB):
        _proj_rows(sub, x_ref, ng_ref, w1_ref, qlg_ref, kvlg_ref, wq_ref, wqp_ref, wknT_ref,
                   wv_ref, qshift_ref, gq_ref, gqp_ref, gk_ref, gkp_ref, cosq_ref, sinq_ref,
                   cost_ref, sint_ref, fc_ref, q_out, kt_out, v_out, wsr_ref, wsi_ref)
    for s1 in range(RADIX):
        for g in range(N_GROUPS):
            half, j = divmod(g, GROUPS_PER_HALF)
            off = half * HALF_LANES + s1 * HALF_WIDTH + j * GROUP_DIM
            pick = pl.ds(s1, S2_PER_TILE, stride=PITCH)
            wr_out[0, :, off:off + GROUP_DIM] = wsr_ref[g, pick, :].astype(BF16)
            wi_out[0, :, off:off + GROUP_DIM] = wsi_ref[g, pick, :].astype(BF16)


def _proj_rows(sub, x_ref, ng_ref, w1_ref, qlg_ref, kvlg_ref, wq_ref, wqp_ref, wknT_ref, wv_ref,
               qshift_ref, gq_ref, gqp_ref, gk_ref, gkp_ref, cosq_ref, sinq_ref, cost_ref,
               sint_ref, fc_ref, q_out, kt_out, v_out, wsr_ref, wsi_ref):
    rows = slice(sub * PROJ_SUB, (sub + 1) * PROJ_SUB)
    x = x_ref[rows, :]
    hb = (x * _rms(x, -1) * ng_ref[...]).astype(BF16)
    p = _dot(hb, w1_ref[...])
    cq = p[:, 0:Q_RANK]
    ckv = p[:, Q_RANK:Q_RANK + KV_RANK]
    kpe = p[:, 512:LAT_WIDTH]
    u = p[:, LAT_WIDTH:LAT_WIDTH + FOURIER_WIDTH]

    ch_scale = GROUP_DIM ** -0.5
    fc = fc_ref[...].astype(BF16)
    s2_per_sub = PROJ_SUB // RADIX
    for g in range(N_GROUPS):
        ug = u[:, g * GROUP_DIM:(g + 1) * GROUP_DIM].astype(BF16)
        wg = _dot(ug, fc) * ch_scale
        for t in range(s2_per_sub):
            slab = slice(t * RADIX, (t + 1) * RADIX)
            slot = (sub * s2_per_sub + t) * PITCH
            wsr_ref[g, slot:slot + RADIX, :] = wg[slab, :GROUP_DIM]
            wsi_ref[g, slot:slot + RADIX, :] = wg[slab, GROUP_DIM:]

    cqn = (cq * _rms(cq, -1) * qlg_ref[...]).astype(BF16)
    ckvn_f = ckv * _rms(ckv, -1) * kvlg_ref[...]
    ckvn = ckvn_f.astype(BF16)

    v_out[rows, :] = _dot(ckvn, wv_ref[...]).astype(BF16)

    q_raw = _dot(cqn, wq_ref[...])
    q_par = _dot(cqn, wqp_ref[...])
    gcq = gq_ref[...] * cosq_ref[rows, :]
    gsq = gqp_ref[...] * sinq_ref[rows, :]
    for h in range(N_HEADS):
        sl = slice(h * HEAD_PAD, (h + 1) * HEAD_PAD)
        qh = q_raw[:, sl]
        r = lax.rsqrt(jnp.sum(qh * qh, axis=-1, keepdims=True) * (1.0 / QK_DIM) + NORM_EPS)
        q_out[rows, sl] = ((qh * gcq + q_par[:, sl] * gsq) * (r * Q_SCALE)
                           + qshift_ref[...]).astype(BF16)

    ckvn_t = ckvn_f.T.astype(BF16)
    kn_t = _dot(wknT_ref[...], ckvn_t)
    kp = kpe.T[0:QK_ROPE, :]
    kp_sw = jnp.concatenate([kp[QK_ROPE // 2:], kp[:QK_ROPE // 2]], axis=0)
    gk = gk_ref[...]
    rope = (kp * (gk[QK_NOPE:] * cost_ref[:, rows])
            + kp_sw * (gkp_ref[...] * sint_ref[:, rows]))
    ss_pe = jnp.sum(kp * kp, axis=0, keepdims=True)
    pad_row = lax.broadcasted_iota(jnp.int32, (HEAD_PAD - QK_DIM, PROJ_SUB), 0)
    ones_then_zeros = jnp.where(pad_row == 0, 1.0, 0.0).astype(BF16)
    for h in range(N_HEADS):
        kn = kn_t[h * QK_NOPE:(h + 1) * QK_NOPE, :]
        ss = jnp.sum(kn * kn, axis=0, keepdims=True) + ss_pe
        r = lax.rsqrt(ss * (1.0 / QK_DIM) + NORM_EPS)
        base = h * HEAD_PAD
        kt_out[0, base:base + QK_NOPE, rows] = (kn * gk[:QK_NOPE] * r).astype(BF16)
        kt_out[0, base + QK_NOPE:base + QK_DIM, rows] = (rope * r).astype(BF16)
        kt_out[0, base + QK_DIM:base + HEAD_PAD, rows] = ones_then_zeros


def _layer_spec(layer, shape):
    return pl.BlockSpec((None,) + shape, lambda *_: (layer,) + (0,) * len(shape))


def _proj_call(layer, x2, ng, w1, qlg, kvlg, wq, wqp, wknT, wv, qshift, gq, gqp, gk, gkp,
               cosq, sinq, cost, sint, fc):
    tm = PROJ_TM
    spb = SEQ // tm
    const = lambda shape: pl.BlockSpec(shape, lambda i: (0,) * len(shape))
    per_layer = lambda *shape: _layer_spec(layer, shape)
    in_specs = [
        pl.BlockSpec((tm, D_MODEL), lambda i: (i, 0)),
        per_layer(1, D_MODEL),
        per_layer(D_MODEL, LAT_WIDTH + FOURIER_WIDTH),
        per_layer(1, Q_RANK),
        per_layer(1, KV_RANK),
        per_layer(Q_RANK, N_HEADS * HEAD_PAD),
        per_layer(Q_RANK, N_HEADS * HEAD_PAD),
        per_layer(N_HEADS * QK_NOPE, KV_RANK),
        per_layer(KV_RANK, N_HEADS * V_DIM),
        per_layer(1, HEAD_PAD),
        per_layer(1, HEAD_PAD),
        per_layer(1, HEAD_PAD),
        per_layer(QK_DIM, 1),
        per_layer(QK_ROPE, 1),
        pl.BlockSpec((tm, HEAD_PAD), lambda i: (i % spb, 0)),
        pl.BlockSpec((tm, HEAD_PAD), lambda i: (i % spb, 0)),
        pl.BlockSpec((QK_ROPE, tm), lambda i: (0, i % spb)),
        pl.BlockSpec((QK_ROPE, tm), lambda i: (0, i % spb)),
        const((GROUP_DIM, 2 * GROUP_DIM)),
    ]
    out_shape = [
        jax.ShapeDtypeStruct((TOKENS, N_HEADS * HEAD_PAD), BF16),
        jax.ShapeDtypeStruct((BATCH, N_HEADS * HEAD_PAD, SEQ), BF16),
        jax.ShapeDtypeStruct((TOKENS, N_HEADS * V_DIM), BF16),
        jax.ShapeDtypeStruct((BATCH, RADIX, N_HALVES * HALF_LANES), BF16),
        jax.ShapeDtypeStruct((BATCH, RADIX, N_HALVES * HALF_LANES), BF16),
    ]
    dft_in_spec = pl.BlockSpec((1, S2_PER_TILE, N_HALVES * HALF_LANES),
                               lambda i: (i // spb, i % spb, 0))
    out_specs = [
        pl.BlockSpec((tm, N_HEADS * HEAD_PAD), lambda i: (i, 0)),
        pl.BlockSpec((1, N_HEADS * HEAD_PAD, tm), lambda i: (i // spb, 0, i % spb)),
        pl.BlockSpec((tm, N_HEADS * V_DIM), lambda i: (i, 0)),
        dft_in_spec,
        dft_in_spec,
    ]
    regroup = pltpu.VMEM((N_GROUPS, S2_PER_TILE * PITCH, GROUP_DIM), F32)
    return pl.pallas_call(
        _proj_body,
        grid=(TOKENS // tm,),
        in_specs=in_specs,
        out_specs=out_specs,
        out_shape=out_shape,
        scratch_shapes=[regroup, regroup],
        compiler_params=pltpu.CompilerParams(
            dimension_semantics=("arbitrary",), vmem_limit_bytes=VMEM_LIMIT),
        name="proj",
    )(x2, ng, w1, qlg, kvlg, wq, wqp, wknT, wv, qshift, gq, gqp, gk, gkp,
      cosq, sinq, cost, sint, fc)


def _attn_values(v_ref, vx_ref):
    lane = lax.broadcasted_iota(jnp.int32, (SEQ, HEAD_PAD), 1)
    for pair in range(ATTN_HEADS // 2):
        vp = v_ref[0, :, pair * HEAD_PAD:(pair + 1) * HEAD_PAD]
        one = jnp.ones_like(vp)
        vx_ref[2 * pair] = jnp.where(lane < V_DIM, vp, one)
        vx_ref[2 * pair + 1] = jnp.where(lane < V_DIM, one, vp)


def _attn_store(outs, o_ref, rows):
    lane = lax.broadcasted_iota(jnp.int32, outs[0].shape, 1)
    for pair in range(ATTN_HEADS // 2):
        even, odd = outs[2 * pair], outs[2 * pair + 1]
        num = jnp.where(lane < V_DIM, even, odd)
        den = jnp.where(lane < V_DIM, pltpu.roll(even, V_DIM, 1), pltpu.roll(odd, V_DIM, 1))
        o_ref[0, rows, pair * HEAD_PAD:(pair + 1) * HEAD_PAD] = (num / den).astype(BF16)


def _attn_exact_body(q_ref, kt_ref, v_ref, o_ref, vx_ref, p_ref, s_ref):
    _attn_values(v_ref, vx_ref)
    lanes = lambda u: slice(u * HEAD_PAD, (u + 1) * HEAD_PAD)

    def tile(i, carry):
        rows = pl.ds(pl.multiple_of(i * ATTN_TQ, ATTN_TQ), ATTN_TQ)
        maxes = [None] * ATTN_HEADS
        outs = [None] * ATTN_HEADS
        for t in range(ATTN_HEADS + 2):
            if t < ATTN_HEADS:
                s = _dot(q_ref[0, rows, lanes(t)], kt_ref[0, lanes(t), :])
                s_ref[t % 2] = s
                maxes[t] = jnp.max(s, axis=-1, keepdims=True)
            u = t - 1
            if 0 <= u < ATTN_HEADS:
                p_ref[u % 2] = jnp.exp2(s_ref[u % 2] - maxes[u]).astype(BF16)
            u = t - 2
            if 0 <= u < ATTN_HEADS:
                outs[u] = _dot(p_ref[u % 2], vx_ref[u])
        _attn_store(outs, o_ref, rows)
        return carry

    lax.fori_loop(0, SEQ // ATTN_TQ, tile, 0)


def _attn_bounded_body(q_ref, kt_ref, v_ref, o_ref, vx_ref, p_ref):
    _attn_values(v_ref, vx_ref)
    lanes = lambda u: slice(u * HEAD_PAD, (u + 1) * HEAD_PAD)

    def tile(i, carry):
        rows = pl.ds(pl.multiple_of(i * ATTN_TQ, ATTN_TQ), ATTN_TQ)
        outs = [None] * ATTN_HEADS
        for t in range(ATTN_HEADS + 1):
            if t < ATTN_HEADS:
                s = _dot(q_ref[0, rows, lanes(t)], kt_ref[0, lanes(t), :])
                p_ref[t % 2] = jnp.exp2(s).astype(BF16)
            u = t - 1
            if 0 <= u < ATTN_HEADS:
                outs[u] = _dot(p_ref[u % 2], vx_ref[u])
        _attn_store(outs, o_ref, rows)
        return carry

    lax.fori_loop(0, SEQ // ATTN_TQ, tile, 0)


def _attn_call(q3, kt, v3, bounded):
    tq = ATTN_TQ
    hps = ATTN_HEADS
    scratch = [pltpu.VMEM((hps, SEQ, HEAD_PAD), BF16), pltpu.VMEM((2, tq, SEQ), BF16)]
    if not bounded:
        scratch.append(pltpu.VMEM((2, tq, SEQ), F32))
    return pl.pallas_call(
        _attn_bounded_body if bounded else _attn_exact_body,
        grid=(BATCH, N_HEADS // hps),
        in_specs=[
            pl.BlockSpec((1, SEQ, hps * HEAD_PAD), lambda b, j: (b, 0, j)),
            pl.BlockSpec((1, hps * HEAD_PAD, SEQ), lambda b, j: (b, j, 0)),
            pl.BlockSpec((1, SEQ, hps * V_DIM), lambda b, j: (b, 0, j)),
        ],
        out_specs=pl.BlockSpec((1, SEQ, hps * V_DIM), lambda b, j: (b, 0, j)),
        out_shape=jax.ShapeDtypeStruct((BATCH, SEQ, ATTN_WIDTH), BF16),
        scratch_shapes=scratch,
        compiler_params=pltpu.CompilerParams(
            dimension_semantics=("arbitrary", "arbitrary"),
            vmem_limit_bytes=ATTN_VMEM_LIMIT),
        name="attn_bounded" if bounded else "attn_exact",
    )(q3, kt, v3)


def _dft_body(wr_ref, wi_ref, cs_ref, tb_ref, f_out, xr_ref, xi_ref, y_ref):
    cs = cs_ref[...].astype(BF16)
    for c in range(RADIX // DFT_S1_CHUNK):
        lanes = slice(c * DFT_S1_CHUNK * HALF_WIDTH, (c + 1) * DFT_S1_CHUNK * HALF_WIDTH)
        w = jnp.concatenate([wr_ref[0, :, lanes], wi_ref[0, :, lanes]], axis=0)
        x = _dot(cs, w)
        xr = x[:RADIX]
        xi = x[RADIX:]
        for t in range(DFT_S1_CHUNK):
            s1 = c * DFT_S1_CHUNK + t
            for j in range(GROUPS_PER_HALF):
                sub = slice(t * HALF_WIDTH + j * GROUP_DIM, t * HALF_WIDTH + (j + 1) * GROUP_DIM)
                xr_ref[j, s1 * PITCH:s1 * PITCH + RADIX, :] = xr[:, sub]
                xi_ref[j, s1 * PITCH:s1 * PITCH + RADIX, :] = xi[:, sub]
    for k2 in range(RADIX):
        pick = pl.ds(k2, RADIX, stride=PITCH)
        zr = jnp.concatenate([xr_ref[j, pick, :] for j in range(GROUPS_PER_HALF)], axis=1)
        zi = jnp.concatenate([xi_ref[j, pick, :] for j in range(GROUPS_PER_HALF)], axis=1)
        z = jnp.concatenate([zr, zi], axis=0).astype(BF16)
        y = _dot(tb_ref[k2].astype(BF16), z)
        for j in range(GROUPS_PER_HALF):
            y_ref[j, pick, :] = y[:, j * GROUP_DIM:(j + 1) * GROUP_DIM]
    for k1 in range(RADIX):
        rows = slice(k1 * PITCH, k1 * PITCH + RADIX)
        for j in range(GROUPS_PER_HALF):
            f_out[0, k1, :, j * GROUP_DIM:(j + 1) * GROUP_DIM] = y_ref[j, rows, :].astype(BF16)


def _dft_call(wr, wi, cs, tb):
    spec = pl.BlockSpec((1, RADIX, HALF_LANES), lambda b, h: (b, 0, h))
    regroup = pltpu.VMEM((GROUPS_PER_HALF, RADIX * PITCH, GROUP_DIM), F32)
    return pl.pallas_call(
        _dft_body,
        grid=(BATCH, N_HALVES),
        in_specs=[spec, spec,
                  pl.BlockSpec((2 * RADIX, 2 * RADIX), lambda b, h: (0, 0)),
                  pl.BlockSpec((RADIX, RADIX, 2 * RADIX), lambda b, h: (0, 0, 0))],
        out_specs=pl.BlockSpec((1, RADIX, RADIX, HALF_WIDTH), lambda b, h: (b, 0, 0, h)),
        out_shape=jax.ShapeDtypeStruct((BATCH, RADIX, RADIX, FOURIER_WIDTH), BF16),
        scratch_shapes=[regroup, regroup, regroup],
        compiler_params=pltpu.CompilerParams(
            dimension_semantics=("arbitrary", "arbitrary"), vmem_limit_bytes=VMEM_LIMIT),
        name="dft",
    )(wr, wi, cs, tb)


def _merge_body(x_ref, oa_ref, f_ref, ng_ref, wg_ref, wa_ref, wf_ref, bm_ref, wo_ref, out_ref):
    def sub_tile(i, carry):
        rows = pl.ds(pl.multiple_of(i * MERGE_SUB, MERGE_SUB), MERGE_SUB)
        x = x_ref[rows, :]
        hb = (x * _rms(x, -1) * ng_ref[...]).astype(BF16)
        gates = _dot(hb, wg_ref[...])
        z_a = gates[:, 0:ATTN_WIDTH]
        z_f = gates[:, ATTN_WIDTH:ATTN_WIDTH + FOURIER_WIDTH]
        g_a = gates[:, 1024:1024 + D_MODEL]
        g_f = gates[:, 1024 + D_MODEL:]
        oa = oa_ref[rows, :].astype(F32)
        f = f_ref[rows, :].astype(F32)
        ya = _dot((oa * (z_a * jax.nn.sigmoid(z_a))).astype(BF16), wa_ref[...])
        yf = _dot((f * (z_f * jax.nn.sigmoid(z_f))).astype(BF16), wf_ref[...])
        bm = bm_ref[...]
        m = jax.nn.sigmoid(g_a + bm[0:1]) * ya + jax.nn.sigmoid(g_f + bm[1:2]) * yf
        out_ref[rows, :] = x + _dot(m.astype(BF16), wo_ref[...])
        return carry

    lax.fori_loop(0, MERGE_TM // MERGE_SUB, sub_tile, 0)


def _merge_call(layer, x2, oa, f, ng, wg, wa, wf, bm, wo):
    tm = MERGE_TM
    per_layer = lambda *shape: pl.BlockSpec(
        (None,) + shape, lambda i: (layer,) + (0,) * len(shape), pipeline_mode=pl.Buffered(1))
    return pl.pallas_call(
        _merge_body,
        grid=(TOKENS // tm,),
        in_specs=[
            pl.BlockSpec((tm, D_MODEL), lambda i: (i, 0)),
            pl.BlockSpec((tm, ATTN_WIDTH), lambda i: (i, 0)),
            pl.BlockSpec((tm, FOURIER_WIDTH), lambda i: (i, 0)),
            per_layer(1, D_MODEL),
            per_layer(D_MODEL, 3 * D_MODEL),
            per_layer(ATTN_WIDTH, D_MODEL),
            per_layer(FOURIER_WIDTH, D_MODEL),
            per_layer(2, D_MODEL),
            per_layer(D_MODEL, D_MODEL),
        ],
        out_specs=pl.BlockSpec((tm, D_MODEL), lambda i: (i, 0)),
        out_shape=jax.ShapeDtypeStruct((TOKENS, D_MODEL), F32),
        compiler_params=pltpu.CompilerParams(
            dimension_semantics=("arbitrary",), vmem_limit_bytes=VMEM_LIMIT),
        name="merge",
    )(x2, oa, f, ng, wg, wa, wf, bm, wo)


def _dft_tables():
    def cs(num, den):
        ang = 2.0 * np.pi * (num % den).astype(np.float64) / den
        return np.cos(ang), np.sin(ang)

    c = np.arange(GROUP_DIM)
    cc, sc = cs(np.outer(c, c), GROUP_DIM)
    fc = np.concatenate([cc, -sc], axis=1)

    k = np.arange(RADIX)
    ca, sa = cs(np.outer(k, k), RADIX)
    stage_a = np.block([[ca, sa], [-sa, ca]]) / 8.0

    k2 = k[:, None, None]
    k1 = k[None, :, None]
    s1 = k[None, None, :]
    cb, sb = cs(s1 * (RADIX * k1 + k2), SEQ)
    stage_b = np.concatenate([cb, sb], axis=2) / 8.0
    return jnp.asarray(fc, F32), jnp.asarray(stage_a, F32), jnp.asarray(stage_b, F32)


def _rope_tables():
    half = QK_ROPE // 2
    inv_freq = ROPE_THETA ** (-np.arange(half, dtype=np.float64) / half)
    ang = np.arange(SEQ, dtype=np.float64)[:, None] * inv_freq[None, :]
    cos, sin = np.cos(ang), np.sin(ang)
    cos32 = np.concatenate([cos, cos], axis=1)
    sin32 = np.concatenate([-sin, sin], axis=1)
    pad = np.zeros((SEQ, HEAD_PAD - QK_DIM))
    cosq = np.concatenate([np.ones((SEQ, QK_NOPE)), cos32, pad], axis=1)
    sinq = np.concatenate([np.zeros((SEQ, QK_NOPE)), sin32, pad], axis=1)
    return tuple(jnp.asarray(t, F32) for t in (cosq, sinq, cos32.T, sin32.T))


def _swap_halves(a):
    half = a.shape[-1] // 2
    return jnp.concatenate([a[..., half:], a[..., :half]], axis=-1)


def _pad_last(a, before, after):
    return jnp.pad(a, [(0, 0)] * (a.ndim - 1) + [(before, after)])


def _prepare_params(norm_g, w_in, q_latent_g, kv_latent_g, w_uq, w_ukv, q_head_g, k_head_g,
                    w_attn_proj, w_fourier_proj, b_merge, w_out):
    c0 = Q_RANK + KV_RANK + QK_ROPE
    u0 = c0 + ATTN_WIDTH
    u1 = u0 + FOURIER_WIDTH
    w1 = jnp.concatenate([_pad_last(w_in[:, :, :c0], 0, LAT_WIDTH - c0), w_in[:, :, u0:u1]],
                         axis=2).astype(BF16)
    wg = jnp.concatenate([w_in[:, :, c0:u0], w_in[:, :, u1:]], axis=2).astype(BF16)

    lead = w_uq.shape[:2]
    wq4 = w_uq.reshape(lead + (N_HEADS, QK_DIM))
    wq = _pad_last(wq4, 0, HEAD_PAD - QK_DIM).reshape(lead + (-1,)).astype(BF16)
    wqp = _pad_last(_swap_halves(wq4[..., QK_NOPE:]), QK_NOPE, HEAD_PAD - QK_DIM)
    wqp = wqp.reshape(lead + (-1,)).astype(BF16)

    wkv4 = w_ukv.reshape(lead + (N_HEADS, QK_NOPE + V_DIM))
    wknT = jnp.swapaxes(wkv4[..., :QK_NOPE].reshape(lead + (-1,)), 1, 2).astype(BF16)
    wv = wkv4[..., QK_NOPE:].reshape(lead + (-1,)).astype(BF16)

    score_bound = (QK_DIM * Q_SCALE * jnp.max(jnp.abs(q_head_g), axis=1)
                   * jnp.max(jnp.abs(k_head_g), axis=1))
    shift_lane = (np.arange(HEAD_PAD) == QK_DIM).astype(np.float32)
    qshift = -score_bound[:, None, None] * shift_lane[None, None, :]

    gq = _pad_last(q_head_g, 0, HEAD_PAD - QK_DIM)[:, None, :]
    gqp = _pad_last(_swap_halves(q_head_g[:, QK_NOPE:]), QK_NOPE, HEAD_PAD - QK_DIM)[:, None, :]
    gk = k_head_g[:, :, None]
    gkp = _swap_halves(k_head_g[:, QK_NOPE:])[:, :, None]
    proj = (norm_g[:, None, :], w1, q_latent_g[:, None, :], kv_latent_g[:, None, :],
            wq, wqp, wknT, wv, qshift, gq, gqp, gk, gkp)
    merge = (norm_g[:, None, :], wg, w_attn_proj.astype(BF16), w_fourier_proj.astype(BF16),
             b_merge, w_out.astype(BF16))
    return proj, merge, score_bound


def kernel(x, norm_g, w_in, q_latent_g, kv_latent_g, w_uq, w_ukv, q_head_g, k_head_g,
           w_attn_proj, w_fourier_proj, b_merge, w_out):
    fc, stage_a, stage_b = _dft_tables()
    rope = _rope_tables()
    proj_params, merge_params, score_bound = _prepare_params(
        norm_g, w_in, q_latent_g, kv_latent_g, w_uq, w_ukv, q_head_g, k_head_g,
        w_attn_proj, w_fourier_proj, b_merge, w_out)

    x2 = x.reshape(TOKENS, D_MODEL)
    for layer in range(DEPTH):
        q, kt, v, wr, wi = _proj_call(layer, x2, *proj_params, *rope, fc)
        oa = lax.cond(score_bound[layer] < MAX_SAFE_SCORE_BOUND,
                      lambda *a: _attn_call(*a, bounded=True),
                      lambda *a: _attn_call(*a, bounded=False),
                      q.reshape(BATCH, SEQ, N_HEADS * HEAD_PAD), kt,
                      v.reshape(BATCH, SEQ, ATTN_WIDTH))
        f = _dft_call(wr, wi, stage_a, stage_b)
        x2 = _merge_call(layer, x2, oa.reshape(TOKENS, ATTN_WIDTH),
                         f.reshape(TOKENS, FOURIER_WIDTH), *merge_params)
    return x2.reshape(BATCH, SEQ, D_MODEL)
```

```python
import math

import numpy as np
import jax
import jax.numpy as jnp
from jax import lax
from jax.experimental import pallas as pl
from jax.experimental.pallas import tpu as pltpu

D_MODEL = 1024
BATCH = 4
SEQ = 4096
DEPTH = 4
N_HEADS = 8
QK_NOPE = 64
QK_ROPE = 32
QK_DIM = QK_NOPE + QK_ROPE
V_DIM = 64
HEAD_PAD = 128
ATTN_WIDTH = N_HEADS * V_DIM
Q_RANK = 256
KV_RANK = 256
ROPE_THETA = 10000.0
FOURIER_WIDTH = 512
GROUP_DIM = 128
N_GROUPS = 4
NORM_EPS = 1e-6
RADIX = 64
LAT_WIDTH = 640

F32 = jnp.float32
BF16 = jnp.bfloat16

VMEM_LIMIT = 48 * 1024 * 1024
ATTN_VMEM_LIMIT = 56 * 1024 * 1024

TOKENS = BATCH * SEQ
PROJ_TM = 1024
PROJ_SUB = 512
ATTN_TQ_BOUNDED = 512
ATTN_TQ_EXACT = 256
ATTN_HEADS = 4
MAX_SAFE_SCORE_BOUND = 40.0
MERGE_TM = 1024
MERGE_SUB = 1024

S2_PER_TILE = PROJ_TM // RADIX
N_HALVES = 2
GROUPS_PER_HALF = N_GROUPS // N_HALVES
HALF_WIDTH = FOURIER_WIDTH // N_HALVES
HALF_LANES = RADIX * HALF_WIDTH
PITCH = RADIX + 8
DFT_S1_CHUNK = 8

Q_SCALE = (QK_DIM ** -0.5) * math.log2(math.e)


def _dot(a, b):
    return jnp.dot(a, b, preferred_element_type=F32)


def _rms(x, axis):
    return lax.rsqrt(jnp.mean(x * x, axis=axis, keepdims=True) + NORM_EPS)


def _proj_body(x_ref, ng_ref, w1_ref, qlg_ref, kvlg_ref, wq_ref, wqp_ref, wknT_ref, wv_ref,
               qshift_ref, gq_ref, gqp_ref, gk_ref, gkp_ref, cosq_ref, sinq_ref, cost_ref,
               sint_ref, fc_ref, q_out, kt_out, v_out, wr_out, wi_out, wsr_ref, wsi_ref):
    for sub in range(PROJ_TM // PROJ_SUB):
        _proj_rows(sub, x_ref, ng_ref, w1_ref, qlg_ref, kvlg_ref, wq_ref, wqp_ref, wknT_ref,
                   wv_ref, qshift_ref, gq_ref, gqp_ref, gk_ref, gkp_ref, cosq_ref, sinq_ref,
                   cost_ref, sint_ref, fc_ref, q_out, kt_out, v_out, wsr_ref, wsi_ref)
    for s1 in range(RADIX):
        for g in range(N_GROUPS):
            half, j = divmod(g, GROUPS_PER_HALF)
            off = half * HALF_LANES + s1 * HALF_WIDTH + j * GROUP_DIM
            pick = pl.ds(s1, S2_PER_TILE, stride=PITCH)
            wr_out[0, :, off:off + GROUP_DIM] = wsr_ref[g, pick, :].astype(BF16)
            wi_out[0, :, off:off + GROUP_DIM] = wsi_ref[g, pick, :].astype(BF16)


def _proj_rows(sub, x_ref, ng_ref, w1_ref, qlg_ref, kvlg_ref, wq_ref, wqp_ref, wknT_ref, wv_ref,
               qshift_ref, gq_ref, gqp_ref, gk_ref, gkp_ref, cosq_ref, sinq_ref, cost_ref,
               sint_ref, fc_ref, q_out, kt_out, v_out, wsr_ref, wsi_ref):
    rows = slice(sub * PROJ_SUB, (sub + 1) * PROJ_SUB)
    x = x_ref[rows, :]
    hb = (x * _rms(x, -1) * ng_ref[...]).astype(BF16)
    p = _dot(hb, w1_ref[...])
    cq = p[:, 0:Q_RANK]
    ckv = p[:, Q_RANK:Q_RANK + KV_RANK]
    kpe = p[:, 512:LAT_WIDTH]
    u = p[:, LAT_WIDTH:LAT_WIDTH + FOURIER_WIDTH]

    ch_scale = GROUP_DIM ** -0.5
    fc = fc_ref[...].astype(BF16)
    s2_per_sub = PROJ_SUB // RADIX
    for g in range(N_GROUPS):
        ug = u[:, g * GROUP_DIM:(g + 1) * GROUP_DIM].astype(BF16)
        wg = _dot(ug, fc) * ch_scale
        for t in range(s2_per_sub):
            slab = slice(t * RADIX, (t + 1) * RADIX)
            slot = (sub * s2_per_sub + t) * PITCH
            wsr_ref[g, slot:slot + RADIX, :] = wg[slab, :GROUP_DIM]
            wsi_ref[g, slot:slot + RADIX, :] = wg[slab, GROUP_DIM:]

    cqn = (cq * _rms(cq, -1) * qlg_ref[...]).astype(BF16)
    ckvn_f = ckv * _rms(ckv, -1) * kvlg_ref[...]
    ckvn = ckvn_f.astype(BF16)

    v_out[rows, :] = _dot(ckvn, wv_ref[...]).astype(BF16)

    q_raw = _dot(cqn, wq_ref[...])
    q_par = _dot(cqn, wqp_ref[...])
    gcq = gq_ref[...] * cosq_ref[rows, :]
    gsq = gqp_ref[...] * sinq_ref[rows, :]
    for h in range(N_HEADS):
        sl = slice(h * HEAD_PAD, (h + 1) * HEAD_PAD)
        qh = q_raw[:, sl]
        r = lax.rsqrt(jnp.sum(qh * qh, axis=-1, keepdims=True) * (1.0 / QK_DIM) + NORM_EPS)
        q_out[rows, sl] = ((qh * gcq + q_par[:, sl] * gsq) * (r * Q_SCALE)
                           + qshift_ref[...]).astype(BF16)

    ckvn_t = ckvn_f.T.astype(BF16)
    kn_t = _dot(wknT_ref[...], ckvn_t)
    kp = kpe.T[0:QK_ROPE, :]
    kp_sw = jnp.concatenate([kp[QK_ROPE // 2:], kp[:QK_ROPE // 2]], axis=0)
    gk = gk_ref[...]
    rope = (kp * (gk[QK_NOPE:] * cost_ref[:, rows])
            + kp_sw * (gkp_ref[...] * sint_ref[:, rows]))
    ss_pe = jnp.sum(kp * kp, axis=0, keepdims=True)
    pad_row = lax.broadcasted_iota(jnp.int32, (HEAD_PAD - QK_DIM, PROJ_SUB), 0)
    ones_then_zeros = jnp.where(pad_row == 0, 1.0, 0.0).astype(BF16)
    for h in range(N_HEADS):
        kn = kn_t[h * QK_NOPE:(h + 1) * QK_NOPE, :]
        ss = jnp.sum(kn * kn, axis=0, keepdims=True) + ss_pe
        r = lax.rsqrt(ss * (1.0 / QK_DIM) + NORM_EPS)
        base = h * HEAD_PAD
        kt_out[0, base:base + QK_NOPE, rows] = (kn * gk[:QK_NOPE] * r).astype(BF16)
        kt_out[0, base + QK_NOPE:base + QK_DIM, rows] = (rope * r).astype(BF16)
        kt_out[0, base + QK_DIM:base + HEAD_PAD, rows] = ones_then_zeros


def _layer_spec(layer, shape):
    return pl.BlockSpec((None,) + shape, lambda *_: (layer,) + (0,) * len(shape))


def _proj_call(layer, x2, ng, w1, qlg, kvlg, wq, wqp, wknT, wv, qshift, gq, gqp, gk, gkp,
               cosq, sinq, cost, sint, fc):
    tm = PROJ_TM
    spb = SEQ // tm
    const = lambda shape: pl.BlockSpec(shape, lambda i: (0,) * len(shape))
    per_layer = lambda *shape: _layer_spec(layer, shape)
    in_specs = [
        pl.BlockSpec((tm, D_MODEL), lambda i: (i, 0)),
        per_layer(1, D_MODEL),
        per_layer(D_MODEL, LAT_WIDTH + FOURIER_WIDTH),
        per_layer(1, Q_RANK),
        per_layer(1, KV_RANK),
        per_layer(Q_RANK, N_HEADS * HEAD_PAD),
        per_layer(Q_RANK, N_HEADS * HEAD_PAD),
        per_layer(N_HEADS * QK_NOPE, KV_RANK),
        per_layer(KV_RANK, N_HEADS * V_DIM),
        per_layer(1, HEAD_PAD),
        per_layer(1, HEAD_PAD),
        per_layer(1, HEAD_PAD),
        per_layer(QK_DIM, 1),
        per_layer(QK_ROPE, 1),
        pl.BlockSpec((tm, HEAD_PAD), lambda i: (i % spb, 0)),
        pl.BlockSpec((tm, HEAD_PAD), lambda i: (i % spb, 0)),
        pl.BlockSpec((QK_ROPE, tm), lambda i: (0, i % spb)),
        pl.BlockSpec((QK_ROPE, tm), lambda i: (0, i % spb)),
        const((GROUP_DIM, 2 * GROUP_DIM)),
    ]
    out_shape = [
        jax.ShapeDtypeStruct((TOKENS, N_HEADS * HEAD_PAD), BF16),
        jax.ShapeDtypeStruct((BATCH, N_HEADS * HEAD_PAD, SEQ), BF16),
        jax.ShapeDtypeStruct((TOKENS, N_HEADS * V_DIM), BF16),
        jax.ShapeDtypeStruct((BATCH, RADIX, N_HALVES * HALF_LANES), BF16),
        jax.ShapeDtypeStruct((BATCH, RADIX, N_HALVES * HALF_LANES), BF16),
    ]
    dft_in_spec = pl.BlockSpec((1, S2_PER_TILE, N_HALVES * HALF_LANES),
                               lambda i: (i // spb, i % spb, 0))
    out_specs = [
        pl.BlockSpec((tm, N_HEADS * HEAD_PAD), lambda i: (i, 0)),
        pl.BlockSpec((1, N_HEADS * HEAD_PAD, tm), lambda i: (i // spb, 0, i % spb)),
        pl.BlockSpec((tm, N_HEADS * V_DIM), lambda i: (i, 0)),
        dft_in_spec,
        dft_in_spec,
    ]
    regroup = pltpu.VMEM((N_GROUPS, S2_PER_TILE * PITCH, GROUP_DIM), F32)
    return pl.pallas_call(
        _proj_body,
        grid=(TOKENS // tm,),
        in_specs=in_specs,
        out_specs=out_specs,
        out_shape=out_shape,
        scratch_shapes=[regroup, regroup],
        compiler_params=pltpu.CompilerParams(
            dimension_semantics=("arbitrary",), vmem_limit_bytes=VMEM_LIMIT),
        name="proj",
    )(x2, ng, w1, qlg, kvlg, wq, wqp, wknT, wv, qshift, gq, gqp, gk, gkp,
      cosq, sinq, cost, sint, fc)


def _attn_values(v_ref, vx_ref):
    lane = lax.broadcasted_iota(jnp.int32, (SEQ, HEAD_PAD), 1)
    for pair in range(ATTN_HEADS // 2):
        vp = v_ref[0, :, pair * HEAD_PAD:(pair + 1) * HEAD_PAD]
        one = jnp.ones_like(vp)
        vx_ref[2 * pair] = jnp.where(lane < V_DIM, vp, one)
        vx_ref[2 * pair + 1] = jnp.where(lane < V_DIM, one, vp)


def _attn_store(outs, o_ref, rows):
    lane = lax.broadcasted_iota(jnp.int32, outs[0].shape, 1)
    for pair in range(ATTN_HEADS // 2):
        even, odd = outs[2 * pair], outs[2 * pair + 1]
        num = jnp.where(lane < V_DIM, even, odd)
        den = jnp.where(lane < V_DIM, pltpu.roll(even, V_DIM, 1), pltpu.roll(odd, V_DIM, 1))
        o_ref[0, rows, pair * HEAD_PAD:(pair + 1) * HEAD_PAD] = (num / den).astype(BF16)


def _attn_exact_body(q_ref, kt_ref, v_ref, o_ref, vx_ref, p_ref, s_ref):
    _attn_values(v_ref, vx_ref)
    lanes = lambda u: slice(u * HEAD_PAD, (u + 1) * HEAD_PAD)
    tq = p_ref.shape[1]

    def tile(i, carry):
        rows = pl.ds(pl.multiple_of(i * tq, tq), tq)
        maxes = [None] * ATTN_HEADS
        outs = [None] * ATTN_HEADS
        for t in range(ATTN_HEADS + 2):
            if t < ATTN_HEADS:
                s = _dot(q_ref[0, rows, lanes(t)], kt_ref[0, lanes(t), :])
                s_ref[t % 2] = s
                maxes[t] = jnp.max(s, axis=-1, keepdims=True)
            u = t - 1
            if 0 <= u < ATTN_HEADS:
                p_ref[u % 2] = jnp.exp2(s_ref[u % 2] - maxes[u]).astype(BF16)
            u = t - 2
            if 0 <= u < ATTN_HEADS:
                outs[u] = _dot(p_ref[u % 2], vx_ref[u])
        _attn_store(outs, o_ref, rows)
        return carry

    lax.fori_loop(0, SEQ // tq, tile, 0)


def _attn_bounded_body(q_ref, kt_ref, v_ref, o_ref, vx_ref, p_ref):
    _attn_values(v_ref, vx_ref)
    lanes = lambda u: slice(u * HEAD_PAD, (u + 1) * HEAD_PAD)
    tq = p_ref.shape[1]

    def tile(i, carry):
        rows = pl.ds(pl.multiple_of(i * tq, tq), tq)
        outs = [None] * ATTN_HEADS
        for t in range(ATTN_HEADS + 1):
            if t < ATTN_HEADS:
                s = _dot(q_ref[0, rows, lanes(t)], kt_ref[0, lanes(t), :])
                p_ref[t % 2] = jnp.exp2(s).astype(BF16)
            u = t - 1
            if 0 <= u < ATTN_HEADS:
                outs[u] = _dot(p_ref[u % 2], vx_ref[u])
        _attn_store(outs, o_ref, rows)
        return carry

    lax.fori_loop(0, SEQ // tq, tile, 0)


def _attn_call(q3, kt, v3, bounded):
    tq = ATTN_TQ_BOUNDED if bounded else ATTN_TQ_EXACT
    hps = ATTN_HEADS
    scratch = [pltpu.VMEM((hps, SEQ, HEAD_PAD), BF16), pltpu.VMEM((2, tq, SEQ), BF16)]
    if not bounded:
        scratch.append(pltpu.VMEM((2, tq, SEQ), F32))
    return pl.pallas_call(
        _attn_bounded_body if bounded else _attn_exact_body,
        grid=(BATCH, N_HEADS // hps),
        in_specs=[
            pl.BlockSpec((1, SEQ, hps * HEAD_PAD), lambda b, j: (b, 0, j)),
            pl.BlockSpec((1, hps * HEAD_PAD, SEQ), lambda b, j: (b, j, 0)),
            pl.BlockSpec((1, SEQ, hps * V_DIM), lambda b, j: (b, 0, j)),
        ],
        out_specs=pl.BlockSpec((1, SEQ, hps * V_DIM), lambda b, j: (b, 0, j)),
        out_shape=jax.ShapeDtypeStruct((BATCH, SEQ, ATTN_WIDTH), BF16),
        scratch_shapes=scratch,
        compiler_params=pltpu.CompilerParams(
            dimension_semantics=("arbitrary", "arbitrary"),
            vmem_limit_bytes=ATTN_VMEM_LIMIT),
        name="attn_bounded" if bounded else "attn_exact",
    )(q3, kt, v3)


def _dft_body(wr_ref, wi_ref, cs_ref, tb_ref, f_out, xr_ref, xi_ref, y_ref):
    cs = cs_ref[...].astype(BF16)
    for c in range(RADIX // DFT_S1_CHUNK):
        lanes = slice(c * DFT_S1_CHUNK * HALF_WIDTH, (c + 1) * DFT_S1_CHUNK * HALF_WIDTH)
        w = jnp.concatenate([wr_ref[0, :, lanes], wi_ref[0, :, lanes]], axis=0)
        x = _dot(cs, w)
        xr = x[:RADIX]
        xi = x[RADIX:]
        for t in range(DFT_S1_CHUNK):
            s1 = c * DFT_S1_CHUNK + t
            for j in range(GROUPS_PER_HALF):
                sub = slice(t * HALF_WIDTH + j * GROUP_DIM, t * HALF_WIDTH + (j + 1) * GROUP_DIM)
                xr_ref[j, s1 * PITCH:s1 * PITCH + RADIX, :] = xr[:, sub]
                xi_ref[j, s1 * PITCH:s1 * PITCH + RADIX, :] = xi[:, sub]
    for k2 in range(RADIX):
        pick = pl.ds(k2, RADIX, stride=PITCH)
        zr = jnp.concatenate([xr_ref[j, pick, :] for j in range(GROUPS_PER_HALF)], axis=1)
        zi = jnp.concatenate([xi_ref[j, pick, :] for j in range(GROUPS_PER_HALF)], axis=1)
        z = jnp.concatenate([zr, zi], axis=0).astype(BF16)
        y = _dot(tb_ref[k2].astype(BF16), z)
        for j in range(GROUPS_PER_HALF):
            y_ref[j, pick, :] = y[:, j * GROUP_DIM:(j + 1) * GROUP_DIM]
    for k1 in range(RADIX):
        rows = slice(k1 * PITCH, k1 * PITCH + RADIX)
        for j in range(GROUPS_PER_HALF):
            f_out[0, k1, :, j * GROUP_DIM:(j + 1) * GROUP_DIM] = y_ref[j, rows, :].astype(BF16)


def _dft_call(wr, wi, cs, tb):
    spec = pl.BlockSpec((1, RADIX, HALF_LANES), lambda b, h: (b, 0, h))
    regroup = pltpu.VMEM((GROUPS_PER_HALF, RADIX * PITCH, GROUP_DIM), F32)
    return pl.pallas_call(
        _dft_body,
        grid=(BATCH, N_HALVES),
        in_specs=[spec, spec,
                  pl.BlockSpec((2 * RADIX, 2 * RADIX), lambda b, h: (0, 0)),
                  pl.BlockSpec((RADIX, RADIX, 2 * RADIX), lambda b, h: (0, 0, 0))],
        out_specs=pl.BlockSpec((1, RADIX, RADIX, HALF_WIDTH), lambda b, h: (b, 0, 0, h)),
        out_shape=jax.ShapeDtypeStruct((BATCH, RADIX, RADIX, FOURIER_WIDTH), BF16),
        scratch_shapes=[regroup, regroup, regroup],
        compiler_params=pltpu.CompilerParams(
            dimension_semantics=("arbitrary", "arbitrary"), vmem_limit_bytes=VMEM_LIMIT),
        name="dft",
    )(wr, wi, cs, tb)


def _merge_body(x_ref, oa_ref, f_ref, ng_ref, wg_ref, wa_ref, wf_ref, bm_ref, wo_ref, out_ref):
    def sub_tile(i, carry):
        rows = pl.ds(pl.multiple_of(i * MERGE_SUB, MERGE_SUB), MERGE_SUB)
        x = x_ref[rows, :]
        hb = (x * _rms(x, -1) * ng_ref[...]).astype(BF16)
        gates = _dot(hb, wg_ref[...])
        z_a = gates[:, 0:ATTN_WIDTH]
        z_f = gates[:, ATTN_WIDTH:ATTN_WIDTH + FOURIER_WIDTH]
        g_a = gates[:, 1024:1024 + D_MODEL]
        g_f = gates[:, 1024 + D_MODEL:]
        oa = oa_ref[rows, :].astype(F32)
        f = f_ref[rows, :].astype(F32)
        ya = _dot((oa * (z_a * jax.nn.sigmoid(z_a))).astype(BF16), wa_ref[...])
        yf = _dot((f * (z_f * jax.nn.sigmoid(z_f))).astype(BF16), wf_ref[...])
        bm = bm_ref[...]
        m = jax.nn.sigmoid(g_a + bm[0:1]) * ya + jax.nn.sigmoid(g_f + bm[1:2]) * yf
        out_ref[rows, :] = x + _dot(m.astype(BF16), wo_ref[...])
        return carry

    lax.fori_loop(0, MERGE_TM // MERGE_SUB, sub_tile, 0)


def _merge_call(layer, x2, oa, f, ng, wg, wa, wf, bm, wo):
    tm = MERGE_TM
    per_layer = lambda *shape: pl.BlockSpec(
        (None,) + shape, lambda i: (layer,) + (0,) * len(shape), pipeline_mode=pl.Buffered(1))
    return pl.pallas_call(
        _merge_body,
        grid=(TOKENS // tm,),
        in_specs=[
            pl.BlockSpec((tm, D_MODEL), lambda i: (i, 0)),
            pl.BlockSpec((tm, ATTN_WIDTH), lambda i: (i, 0)),
            pl.BlockSpec((tm, FOURIER_WIDTH), lambda i: (i, 0)),
            per_layer(1, D_MODEL),
            per_layer(D_MODEL, 3 * D_MODEL),
            per_layer(ATTN_WIDTH, D_MODEL),
            per_layer(FOURIER_WIDTH, D_MODEL),
            per_layer(2, D_MODEL),
            per_layer(D_MODEL, D_MODEL),
        ],
        out_specs=pl.BlockSpec((tm, D_MODEL), lambda i: (i, 0)),
        out_shape=jax.ShapeDtypeStruct((TOKENS, D_MODEL), F32),
        compiler_params=pltpu.CompilerParams(
            dimension_semantics=("arbitrary",), vmem_limit_bytes=VMEM_LIMIT),
        name="merge",
    )(x2, oa, f, ng, wg, wa, wf, bm, wo)


def _dft_tables():
    def cs(num, den):
        ang = 2.0 * np.pi * (num % den).astype(np.float64) / den
        return np.cos(ang), np.sin(ang)

    c = np.arange(GROUP_DIM)
    cc, sc = cs(np.outer(c, c), GROUP_DIM)
    fc = np.concatenate([cc, -sc], axis=1)

    k = np.arange(RADIX)
    ca, sa = cs(np.outer(k, k), RADIX)
    stage_a = np.block([[ca, sa], [-sa, ca]]) / 8.0

    k2 = k[:, None, None]
    k1 = k[None, :, None]
    s1 = k[None, None, :]
    cb, sb = cs(s1 * (RADIX * k1 + k2), SEQ)
    stage_b = np.concatenate([cb, sb], axis=2) / 8.0
    return jnp.asarray(fc, F32), jnp.asarray(stage_a, F32), jnp.asarray(stage_b, F32)


def _rope_tables():
    half = QK_ROPE // 2
    inv_freq = ROPE_THETA ** (-np.arange(half, dtype=np.float64) / half)
    ang = np.arange(SEQ, dtype=np.float64)[:, None] * inv_freq[None, :]
    cos, sin = np.cos(ang), np.sin(ang)
    cos32 = np.concatenate([cos, cos], axis=1)
    sin32 = np.concatenate([-sin, sin], axis=1)
    pad = np.zeros((SEQ, HEAD_PAD - QK_DIM))
    cosq = np.concatenate([np.ones((SEQ, QK_NOPE)), cos32, pad], axis=1)
    sinq = np.concatenate([np.zeros((SEQ, QK_NOPE)), sin32, pad], axis=1)
    return tuple(jnp.asarray(t, F32) for t in (cosq, sinq, cos32.T, sin32.T))


def _swap_halves(a):
    half = a.shape[-1] // 2
    return jnp.concatenate([a[..., half:], a[..., :half]], axis=-1)


def _pad_last(a, before, after):
    return jnp.pad(a, [(0, 0)] * (a.ndim - 1) + [(before, after)])


def _prepare_params(norm_g, w_in, q_latent_g, kv_latent_g, w_uq, w_ukv, q_head_g, k_head_g,
                    w_attn_proj, w_fourier_proj, b_merge, w_out):
    c0 = Q_RANK + KV_RANK + QK_ROPE
    u0 = c0 + ATTN_WIDTH
    u1 = u0 + FOURIER_WIDTH
    w1 = jnp.concatenate([_pad_last(w_in[:, :, :c0], 0, LAT_WIDTH - c0), w_in[:, :, u0:u1]],
                         axis=2).astype(BF16)
    wg = jnp.concatenate([w_in[:, :, c0:u0], w_in[:, :, u1:]], axis=2).astype(BF16)

    lead = w_uq.shape[:2]
    wq4 = w_uq.reshape(lead + (N_HEADS, QK_DIM))
    wq = _pad_last(wq4, 0, HEAD_PAD - QK_DIM).reshape(lead + (-1,)).astype(BF16)
    wqp = _pad_last(_swap_halves(wq4[..., QK_NOPE:]), QK_NOPE, HEAD_PAD - QK_DIM)
    wqp = wqp.reshape(lead + (-1,)).astype(BF16)

    wkv4 = w_ukv.reshape(lead + (N_HEADS, QK_NOPE + V_DIM))
    wknT = jnp.swapaxes(wkv4[..., :QK_NOPE].reshape(lead + (-1,)), 1, 2).astype(BF16)
    wv = wkv4[..., QK_NOPE:].reshape(lead + (-1,)).astype(BF16)

    score_bound = (QK_DIM * Q_SCALE * jnp.max(jnp.abs(q_head_g), axis=1)
                   * jnp.max(jnp.abs(k_head_g), axis=1))
    shift_lane = (np.arange(HEAD_PAD) == QK_DIM).astype(np.float32)
    qshift = -score_bound[:, None, None] * shift_lane[None, None, :]

    gq = _pad_last(q_head_g, 0, HEAD_PAD - QK_DIM)[:, None, :]
    gqp = _pad_last(_swap_halves(q_head_g[:, QK_NOPE:]), QK_NOPE, HEAD_PAD - QK_DIM)[:, None, :]
    gk = k_head_g[:, :, None]
    gkp = _swap_halves(k_head_g[:, QK_NOPE:])[:, :, None]
    proj = (norm_g[:, None, :], w1, q_latent_g[:, None, :], kv_latent_g[:, None, :],
            wq, wqp, wknT, wv, qshift, gq, gqp, gk, gkp)
    merge = (norm_g[:, None, :], wg, w_attn_proj.astype(BF16), w_fourier_proj.astype(BF16),
             b_merge, w_out.astype(BF16))
    return proj, merge, score_bound


def kernel(x, norm_g, w_in, q_latent_g, kv_latent_g, w_uq, w_ukv, q_head_g, k_head_g,
           w_attn_proj, w_fourier_proj, b_merge, w_out):
    fc, stage_a, stage_b = _dft_tables()
    rope = _rope_tables()
    proj_params, merge_params, score_bound = _prepare_params(
        norm_g, w_in, q_latent_g, kv_latent_g, w_uq, w_ukv, q_head_g, k_head_g,
        w_attn_proj, w_fourier_proj, b_merge, w_out)

    x2 = x.reshape(TOKENS, D_MODEL)
    for layer in range(DEPTH):
        q, kt, v, wr, wi = _proj_call(layer, x2, *proj_params, *rope, fc)
        oa = lax.cond(score_bound[layer] < MAX_SAFE_SCORE_BOUND,
                      lambda *a: _attn_call(*a, bounded=True),
                      lambda *a: _attn_call(*a, bounded=False),
                      q.reshape(BATCH, SEQ, N_HEADS * HEAD_PAD), kt,
                      v.reshape(BATCH, SEQ, ATTN_WIDTH))
        f = _dft_call(wr, wi, stage_a, stage_b)
        x2 = _merge_call(layer, x2, oa.reshape(TOKENS, ATTN_WIDTH),
                         f.reshape(TOKENS, FOURIER_WIDTH), *merge_params)
    return x2.reshape(BATCH, SEQ, D_MODEL)
```

```python
import math

import numpy as np
import jax
import jax.numpy as jnp
from jax import lax
from jax.experimental import pallas as pl
from jax.experimental.pallas import tpu as pltpu

D_MODEL = 1024
BATCH = 4
SEQ = 4096
DEPTH = 4
N_HEADS = 8
QK_NOPE = 64
QK_ROPE = 32
QK_DIM = QK_NOPE + QK_ROPE
V_DIM = 64
HEAD_PAD = 128
ATTN_WIDTH = N_HEADS * V_DIM
Q_RANK = 256
KV_RANK = 256
ROPE_THETA = 10000.0
FOURIER_WIDTH = 512
GROUP_DIM = 128
N_GROUPS = 4
NORM_EPS = 1e-6
RADIX = 64
LAT_WIDTH = 640
PROJ_W_WIDTH = LAT_WIDTH + FOURIER_WIDTH
GATE_WIDTH = 3 * D_MODEL
PROJ_W_BLOCK = -(-GATE_WIDTH // PROJ_W_WIDTH)

F32 = jnp.float32
BF16 = jnp.bfloat16

VMEM_LIMIT = 48 * 1024 * 1024
ATTN_VMEM_LIMIT = 56 * 1024 * 1024

TOKENS = BATCH * SEQ
PROJ_TM = 1024
PROJ_SUB = 512
ATTN_TQ_BOUNDED = 512
ATTN_TQ_EXACT = 256
ATTN_HEADS = 4
MAX_SAFE_SCORE_BOUND = 40.0
MERGE_TM = 1024
MERGE_SUB = 1024

S2_PER_TILE = PROJ_TM // RADIX
N_HALVES = 2
GROUPS_PER_HALF = N_GROUPS // N_HALVES
HALF_WIDTH = FOURIER_WIDTH // N_HALVES
HALF_LANES = RADIX * HALF_WIDTH
PITCH = RADIX + 8
DFT_S1_CHUNK = 8

Q_SCALE = (QK_DIM ** -0.5) * math.log2(math.e)


def _dot(a, b):
    return jnp.dot(a, b, preferred_element_type=F32)


def _rms(x, axis):
    return lax.rsqrt(jnp.mean(x * x, axis=axis, keepdims=True) + NORM_EPS)


def _proj_body(x_ref, ng_ref, w1_ref, qlg_ref, kvlg_ref, wq_ref, wqp_ref, wknT_ref, wv_ref,
               qshift_ref, gq_ref, gqp_ref, gk_ref, gkp_ref, cosq_ref, sinq_ref, cost_ref,
               sint_ref, fc_ref, q_out, kt_out, v_out, wr_out, wi_out, wsr_ref, wsi_ref):
    for sub in range(PROJ_TM // PROJ_SUB):
        _proj_rows(sub, x_ref, ng_ref, w1_ref, qlg_ref, kvlg_ref, wq_ref, wqp_ref, wknT_ref,
                   wv_ref, qshift_ref, gq_ref, gqp_ref, gk_ref, gkp_ref, cosq_ref, sinq_ref,
                   cost_ref, sint_ref, fc_ref, q_out, kt_out, v_out, wsr_ref, wsi_ref)
    for s1 in range(RADIX):
        for g in range(N_GROUPS):
            half, j = divmod(g, GROUPS_PER_HALF)
            off = half * HALF_LANES + s1 * HALF_WIDTH + j * GROUP_DIM
            pick = pl.ds(s1, S2_PER_TILE, stride=PITCH)
            wr_out[0, :, off:off + GROUP_DIM] = wsr_ref[g, pick, :].astype(BF16)
            wi_out[0, :, off:off + GROUP_DIM] = wsi_ref[g, pick, :].astype(BF16)


def _proj_rows(sub, x_ref, ng_ref, w1_ref, qlg_ref, kvlg_ref, wq_ref, wqp_ref, wknT_ref, wv_ref,
               qshift_ref, gq_ref, gqp_ref, gk_ref, gkp_ref, cosq_ref, sinq_ref, cost_ref,
               sint_ref, fc_ref, q_out, kt_out, v_out, wsr_ref, wsi_ref):
    rows = slice(sub * PROJ_SUB, (sub + 1) * PROJ_SUB)
    x = x_ref[rows, :]
    hb = (x * _rms(x, -1) * ng_ref[...]).astype(BF16)
    p = _dot(hb, w1_ref[...])
    cq = p[:, 0:Q_RANK]
    ckv = p[:, Q_RANK:Q_RANK + KV_RANK]
    kpe = p[:, 512:LAT_WIDTH]
    u = p[:, LAT_WIDTH:LAT_WIDTH + FOURIER_WIDTH]

    ch_scale = GROUP_DIM ** -0.5
    fc = fc_ref[...].astype(BF16)
    s2_per_sub = PROJ_SUB // RADIX
    for g in range(N_GROUPS):
        ug = u[:, g * GROUP_DIM:(g + 1) * GROUP_DIM].astype(BF16)
        wg = _dot(ug, fc) * ch_scale
        for t in range(s2_per_sub):
            slab = slice(t * RADIX, (t + 1) * RADIX)
            slot = (sub * s2_per_sub + t) * PITCH
            wsr_ref[g, slot:slot + RADIX, :] = wg[slab, :GROUP_DIM]
            wsi_ref[g, slot:slot + RADIX, :] = wg[slab, GROUP_DIM:]

    cqn = (cq * _rms(cq, -1) * qlg_ref[...]).astype(BF16)
    ckvn_f = ckv * _rms(ckv, -1) * kvlg_ref[...]
    ckvn = ckvn_f.astype(BF16)

    v_out[rows, :] = _dot(ckvn, wv_ref[...]).astype(BF16)

    q_raw = _dot(cqn, wq_ref[...])
    q_par = _dot(cqn, wqp_ref[...])
    gcq = gq_ref[...] * cosq_ref[rows, :]
    gsq = gqp_ref[...] * sinq_ref[rows, :]
    for h in range(N_HEADS):
        sl = slice(h * HEAD_PAD, (h + 1) * HEAD_PAD)
        qh = q_raw[:, sl]
        r = lax.rsqrt(jnp.sum(qh * qh, axis=-1, keepdims=True) * (1.0 / QK_DIM) + NORM_EPS)
        q_out[rows, sl] = ((qh * gcq + q_par[:, sl] * gsq) * (r * Q_SCALE)
                           + qshift_ref[...]).astype(BF16)

    ckvn_t = ckvn_f.T.astype(BF16)
    kn_t = _dot(wknT_ref[...], ckvn_t)
    kp = kpe.T[0:QK_ROPE, :]
    kp_sw = jnp.concatenate([kp[QK_ROPE // 2:], kp[:QK_ROPE // 2]], axis=0)
    gk = gk_ref[...]
    rope = (kp * (gk[QK_NOPE:] * cost_ref[:, rows])
            + kp_sw * (gkp_ref[...] * sint_ref[:, rows]))
    ss_pe = jnp.sum(kp * kp, axis=0, keepdims=True)
    pad_row = lax.broadcasted_iota(jnp.int32, (HEAD_PAD - QK_DIM, PROJ_SUB), 0)
    ones_then_zeros = jnp.where(pad_row == 0, 1.0, 0.0).astype(BF16)
    for h in range(N_HEADS):
        kn = kn_t[h * QK_NOPE:(h + 1) * QK_NOPE, :]
        ss = jnp.sum(kn * kn, axis=0, keepdims=True) + ss_pe
        r = lax.rsqrt(ss * (1.0 / QK_DIM) + NORM_EPS)
        base = h * HEAD_PAD
        kt_out[0, base:base + QK_NOPE, rows] = (kn * gk[:QK_NOPE] * r).astype(BF16)
        kt_out[0, base + QK_NOPE:base + QK_DIM, rows] = (rope * r).astype(BF16)
        kt_out[0, base + QK_DIM:base + HEAD_PAD, rows] = ones_then_zeros


def _layer_spec(layer, shape):
    return pl.BlockSpec((None,) + shape, lambda *_: (layer,) + (0,) * len(shape))


def _proj_call(layer, x2, ng, w1, qlg, kvlg, wq, wqp, wknT, wv, qshift, gq, gqp, gk, gkp,
               cosq, sinq, cost, sint, fc):
    tm = PROJ_TM
    spb = SEQ // tm
    const = lambda shape: pl.BlockSpec(shape, lambda i: (0,) * len(shape))
    per_layer = lambda *shape: _layer_spec(layer, shape)
    in_specs = [
        pl.BlockSpec((tm, D_MODEL), lambda i: (i, 0)),
        per_layer(1, D_MODEL),
        pl.BlockSpec((None, D_MODEL, PROJ_W_WIDTH), lambda i: (layer, 0, PROJ_W_BLOCK)),
        per_layer(1, Q_RANK),
        per_layer(1, KV_RANK),
        per_layer(Q_RANK, N_HEADS * HEAD_PAD),
        per_layer(Q_RANK, N_HEADS * HEAD_PAD),
        per_layer(N_HEADS * QK_NOPE, KV_RANK),
        per_layer(KV_RANK, N_HEADS * V_DIM),
        per_layer(1, HEAD_PAD),
        per_layer(1, HEAD_PAD),
        per_layer(1, HEAD_PAD),
        per_layer(QK_DIM, 1),
        per_layer(QK_ROPE, 1),
        pl.BlockSpec((tm, HEAD_PAD), lambda i: (i % spb, 0)),
        pl.BlockSpec((tm, HEAD_PAD), lambda i: (i % spb, 0)),
        pl.BlockSpec((QK_ROPE, tm), lambda i: (0, i % spb)),
        pl.BlockSpec((QK_ROPE, tm), lambda i: (0, i % spb)),
        const((GROUP_DIM, 2 * GROUP_DIM)),
    ]
    out_shape = [
        jax.ShapeDtypeStruct((TOKENS, N_HEADS * HEAD_PAD), BF16),
        jax.ShapeDtypeStruct((BATCH, N_HEADS * HEAD_PAD, SEQ), BF16),
        jax.ShapeDtypeStruct((TOKENS, N_HEADS * V_DIM), BF16),
        jax.ShapeDtypeStruct((BATCH, RADIX, N_HALVES * HALF_LANES), BF16),
        jax.ShapeDtypeStruct((BATCH, RADIX, N_HALVES * HALF_LANES), BF16),
    ]
    dft_in_spec = pl.BlockSpec((1, S2_PER_TILE, N_HALVES * HALF_LANES),
                               lambda i: (i // spb, i % spb, 0))
    out_specs = [
        pl.BlockSpec((tm, N_HEADS * HEAD_PAD), lambda i: (i, 0)),
        pl.BlockSpec((1, N_HEADS * HEAD_PAD, tm), lambda i: (i // spb, 0, i % spb)),
        pl.BlockSpec((tm, N_HEADS * V_DIM), lambda i: (i, 0)),
        dft_in_spec,
        dft_in_spec,
    ]
    regroup = pltpu.VMEM((N_GROUPS, S2_PER_TILE * PITCH, GROUP_DIM), F32)
    return pl.pallas_call(
        _proj_body,
        grid=(TOKENS // tm,),
        in_specs=in_specs,
        out_specs=out_specs,
        out_shape=out_shape,
        scratch_shapes=[regroup, regroup],
        compiler_params=pltpu.CompilerParams(
            dimension_semantics=("arbitrary",), vmem_limit_bytes=VMEM_LIMIT),
        name="proj",
    )(x2, ng, w1, qlg, kvlg, wq, wqp, wknT, wv, qshift, gq, gqp, gk, gkp,
      cosq, sinq, cost, sint, fc)


def _attn_values(v_ref, vx_ref):
    lane = lax.broadcasted_iota(jnp.int32, (SEQ, HEAD_PAD), 1)
    for pair in range(ATTN_HEADS // 2):
        vp = v_ref[0, :, pair * HEAD_PAD:(pair + 1) * HEAD_PAD]
        one = jnp.ones_like(vp)
        vx_ref[2 * pair] = jnp.where(lane < V_DIM, vp, one)
        vx_ref[2 * pair + 1] = jnp.where(lane < V_DIM, one, vp)


def _attn_store(outs, o_ref, rows):
    lane = lax.broadcasted_iota(jnp.int32, outs[0].shape, 1)
    for pair in range(ATTN_HEADS // 2):
        even, odd = outs[2 * pair], outs[2 * pair + 1]
        num = jnp.where(lane < V_DIM, even, odd)
        den = jnp.where(lane < V_DIM, pltpu.roll(even, V_DIM, 1), pltpu.roll(odd, V_DIM, 1))
        o_ref[0, rows, pair * HEAD_PAD:(pair + 1) * HEAD_PAD] = (num / den).astype(BF16)


def _attn_exact_body(q_ref, kt_ref, v_ref, o_ref, vx_ref, p_ref, s_ref):
    _attn_values(v_ref, vx_ref)
    lanes = lambda u: slice(u * HEAD_PAD, (u + 1) * HEAD_PAD)
    tq = p_ref.shape[1]

    def tile(i, carry):
        rows = pl.ds(pl.multiple_of(i * tq, tq), tq)
        maxes = [None] * ATTN_HEADS
        outs = [None] * ATTN_HEADS
        for t in range(ATTN_HEADS + 2):
            if t < ATTN_HEADS:
                s = _dot(q_ref[0, rows, lanes(t)], kt_ref[0, lanes(t), :])
                s_ref[t % 2] = s
                maxes[t] = jnp.max(s, axis=-1, keepdims=True)
            u = t - 1
            if 0 <= u < ATTN_HEADS:
                p_ref[u % 2] = jnp.exp2(s_ref[u % 2] - maxes[u]).astype(BF16)
            u = t - 2
            if 0 <= u < ATTN_HEADS:
                outs[u] = _dot(p_ref[u % 2], vx_ref[u])
        _attn_store(outs, o_ref, rows)
        return carry

    lax.fori_loop(0, SEQ // tq, tile, 0)


def _attn_bounded_body(q_ref, kt_ref, v_ref, o_ref, vx_ref, p_ref):
    _attn_values(v_ref, vx_ref)
    lanes = lambda u: slice(u * HEAD_PAD, (u + 1) * HEAD_PAD)
    tq = p_ref.shape[1]

    def tile(i, carry):
        rows = pl.ds(pl.multiple_of(i * tq, tq), tq)
        outs = [None] * ATTN_HEADS
        for t in range(ATTN_HEADS + 1):
            if t < ATTN_HEADS:
                s = _dot(q_ref[0, rows, lanes(t)], kt_ref[0, lanes(t), :])
                p_ref[t % 2] = jnp.exp2(s).astype(BF16)
            u = t - 1
            if 0 <= u < ATTN_HEADS:
                outs[u] = _dot(p_ref[u % 2], vx_ref[u])
        _attn_store(outs, o_ref, rows)
        return carry

    lax.fori_loop(0, SEQ // tq, tile, 0)


def _attn_call(q3, kt, v3, bounded):
    tq = ATTN_TQ_BOUNDED if bounded else ATTN_TQ_EXACT
    hps = ATTN_HEADS
    scratch = [pltpu.VMEM((hps, SEQ, HEAD_PAD), BF16), pltpu.VMEM((2, tq, SEQ), BF16)]
    if not bounded:
        scratch.append(pltpu.VMEM((2, tq, SEQ), F32))
    return pl.pallas_call(
        _attn_bounded_body if bounded else _attn_exact_body,
        grid=(BATCH, N_HEADS // hps),
        in_specs=[
            pl.BlockSpec((1, SEQ, hps * HEAD_PAD), lambda b, j: (b, 0, j)),
            pl.BlockSpec((1, hps * HEAD_PAD, SEQ), lambda b, j: (b, j, 0)),
            pl.BlockSpec((1, SEQ, hps * V_DIM), lambda b, j: (b, 0, j)),
        ],
        out_specs=pl.BlockSpec((1, SEQ, hps * V_DIM), lambda b, j: (b, 0, j)),
        out_shape=jax.ShapeDtypeStruct((BATCH, SEQ, ATTN_WIDTH), BF16),
        scratch_shapes=scratch,
        compiler_params=pltpu.CompilerParams(
            dimension_semantics=("arbitrary", "arbitrary"),
            vmem_limit_bytes=ATTN_VMEM_LIMIT),
        name="attn_bounded" if bounded else "attn_exact",
    )(q3, kt, v3)


def _dft_body(wr_ref, wi_ref, cs_ref, tb_ref, f_out, xr_ref, xi_ref, y_ref):
    cs = cs_ref[...].astype(BF16)
    for c in range(RADIX // DFT_S1_CHUNK):
        lanes = slice(c * DFT_S1_CHUNK * HALF_WIDTH, (c + 1) * DFT_S1_CHUNK * HALF_WIDTH)
        w = jnp.concatenate([wr_ref[0, :, lanes], wi_ref[0, :, lanes]], axis=0)
        x = _dot(cs, w)
        xr = x[:RADIX]
        xi = x[RADIX:]
        for t in range(DFT_S1_CHUNK):
            s1 = c * DFT_S1_CHUNK + t
            for j in range(GROUPS_PER_HALF):
                sub = slice(t * HALF_WIDTH + j * GROUP_DIM, t * HALF_WIDTH + (j + 1) * GROUP_DIM)
                xr_ref[j, s1 * PITCH:s1 * PITCH + RADIX, :] = xr[:, sub]
                xi_ref[j, s1 * PITCH:s1 * PITCH + RADIX, :] = xi[:, sub]
    for k2 in range(RADIX):
        pick = pl.ds(k2, RADIX, stride=PITCH)
        zr = jnp.concatenate([xr_ref[j, pick, :] for j in range(GROUPS_PER_HALF)], axis=1)
        zi = jnp.concatenate([xi_ref[j, pick, :] for j in range(GROUPS_PER_HALF)], axis=1)
        z = jnp.concatenate([zr, zi], axis=0).astype(BF16)
        y = _dot(tb_ref[k2].astype(BF16), z)
        for j in range(GROUPS_PER_HALF):
            y_ref[j, pick, :] = y[:, j * GROUP_DIM:(j + 1) * GROUP_DIM]
    for k1 in range(RADIX):
        rows = slice(k1 * PITCH, k1 * PITCH + RADIX)
        for j in range(GROUPS_PER_HALF):
            f_out[0, k1, :, j * GROUP_DIM:(j + 1) * GROUP_DIM] = y_ref[j, rows, :].astype(BF16)


def _dft_call(wr, wi, cs, tb):
    spec = pl.BlockSpec((1, RADIX, HALF_LANES), lambda b, h: (b, 0, h))
    regroup = pltpu.VMEM((GROUPS_PER_HALF, RADIX * PITCH, GROUP_DIM), F32)
    return pl.pallas_call(
        _dft_body,
        grid=(BATCH, N_HALVES),
        in_specs=[spec, spec,
                  pl.BlockSpec((2 * RADIX, 2 * RADIX), lambda b, h: (0, 0)),
                  pl.BlockSpec((RADIX, RADIX, 2 * RADIX), lambda b, h: (0, 0, 0))],
        out_specs=pl.BlockSpec((1, RADIX, RADIX, HALF_WIDTH), lambda b, h: (b, 0, 0, h)),
        out_shape=jax.ShapeDtypeStruct((BATCH, RADIX, RADIX, FOURIER_WIDTH), BF16),
        scratch_shapes=[regroup, regroup, regroup],
        compiler_params=pltpu.CompilerParams(
            dimension_semantics=("arbitrary", "arbitrary"), vmem_limit_bytes=VMEM_LIMIT),
        name="dft",
    )(wr, wi, cs, tb)


def _merge_body(x_ref, oa_ref, f_ref, ng_ref, wg_ref, wa_ref, wf_ref, bm_ref, wo_ref, out_ref):
    def sub_tile(i, carry):
        rows = pl.ds(pl.multiple_of(i * MERGE_SUB, MERGE_SUB), MERGE_SUB)
        x = x_ref[rows, :]
        hb = (x * _rms(x, -1) * ng_ref[...]).astype(BF16)
        gates = _dot(hb, wg_ref[...])
        z_a = gates[:, 0:ATTN_WIDTH]
        z_f = gates[:, ATTN_WIDTH:ATTN_WIDTH + FOURIER_WIDTH]
        g_a = gates[:, 1024:1024 + D_MODEL]
        g_f = gates[:, 1024 + D_MODEL:]
        oa = oa_ref[rows, :].astype(F32)
        f = f_ref[rows, :].astype(F32)
        ya = _dot((oa * (z_a * jax.nn.sigmoid(z_a))).astype(BF16), wa_ref[...])
        yf = _dot((f * (z_f * jax.nn.sigmoid(z_f))).astype(BF16), wf_ref[...])
        bm = bm_ref[...]
        m = jax.nn.sigmoid(g_a + bm[0:1]) * ya + jax.nn.sigmoid(g_f + bm[1:2]) * yf
        out_ref[rows, :] = x + _dot(m.astype(BF16), wo_ref[...])
        return carry

    lax.fori_loop(0, MERGE_TM // MERGE_SUB, sub_tile, 0)


def _merge_call(layer, x2, oa, f, ng, wg, wa, wf, bm, wo):
    tm = MERGE_TM
    per_layer = lambda *shape: pl.BlockSpec(
        (None,) + shape, lambda i: (layer,) + (0,) * len(shape), pipeline_mode=pl.Buffered(1))
    return pl.pallas_call(
        _merge_body,
        grid=(TOKENS // tm,),
        in_specs=[
            pl.BlockSpec((tm, D_MODEL), lambda i: (i, 0)),
            pl.BlockSpec((tm, ATTN_WIDTH), lambda i: (i, 0)),
            pl.BlockSpec((tm, FOURIER_WIDTH), lambda i: (i, 0)),
            per_layer(1, D_MODEL),
            per_layer(D_MODEL, GATE_WIDTH),
            per_layer(ATTN_WIDTH, D_MODEL),
            per_layer(FOURIER_WIDTH, D_MODEL),
            per_layer(2, D_MODEL),
            per_layer(D_MODEL, D_MODEL),
        ],
        out_specs=pl.BlockSpec((tm, D_MODEL), lambda i: (i, 0)),
        out_shape=jax.ShapeDtypeStruct((TOKENS, D_MODEL), F32),
        compiler_params=pltpu.CompilerParams(
            dimension_semantics=("arbitrary",), vmem_limit_bytes=VMEM_LIMIT),
        name="merge",
    )(x2, oa, f, ng, wg, wa, wf, bm, wo)


def _dft_tables():
    def cs(num, den):
        ang = 2.0 * np.pi * (num % den).astype(np.float64) / den
        return np.cos(ang), np.sin(ang)

    c = np.arange(GROUP_DIM)
    cc, sc = cs(np.outer(c, c), GROUP_DIM)
    fc = np.concatenate([cc, -sc], axis=1)

    k = np.arange(RADIX)
    ca, sa = cs(np.outer(k, k), RADIX)
    stage_a = np.block([[ca, sa], [-sa, ca]]) / 8.0

    k2 = k[:, None, None]
    k1 = k[None, :, None]
    s1 = k[None, None, :]
    cb, sb = cs(s1 * (RADIX * k1 + k2), SEQ)
    stage_b = np.concatenate([cb, sb], axis=2) / 8.0
    return jnp.asarray(fc, F32), jnp.asarray(stage_a, F32), jnp.asarray(stage_b, F32)


def _rope_tables():
    half = QK_ROPE // 2
    inv_freq = ROPE_THETA ** (-np.arange(half, dtype=np.float64) / half)
    ang = np.arange(SEQ, dtype=np.float64)[:, None] * inv_freq[None, :]
    cos, sin = np.cos(ang), np.sin(ang)
    cos32 = np.concatenate([cos, cos], axis=1)
    sin32 = np.concatenate([-sin, sin], axis=1)
    pad = np.zeros((SEQ, HEAD_PAD - QK_DIM))
    cosq = np.concatenate([np.ones((SEQ, QK_NOPE)), cos32, pad], axis=1)
    sinq = np.concatenate([np.zeros((SEQ, QK_NOPE)), sin32, pad], axis=1)
    return tuple(jnp.asarray(t, F32) for t in (cosq, sinq, cos32.T, sin32.T))


def _swap_halves(a):
    half = a.shape[-1] // 2
    return jnp.concatenate([a[..., half:], a[..., :half]], axis=-1)


def _pad_last(a, before, after):
    return jnp.pad(a, [(0, 0)] * (a.ndim - 1) + [(before, after)])


def _prepare_params(norm_g, w_in, q_latent_g, kv_latent_g, w_uq, w_ukv, q_head_g, k_head_g,
                    w_attn_proj, w_fourier_proj, b_merge, w_out):
    c0 = Q_RANK + KV_RANK + QK_ROPE
    u0 = c0 + ATTN_WIDTH
    u1 = u0 + FOURIER_WIDTH
    w_all = jnp.concatenate(
        [w_in[:, :, c0:u0], _pad_last(w_in[:, :, u1:], 0, PROJ_W_BLOCK * PROJ_W_WIDTH - GATE_WIDTH),
         _pad_last(w_in[:, :, :c0], 0, LAT_WIDTH - c0), w_in[:, :, u0:u1]], axis=2).astype(BF16)

    lead = w_uq.shape[:2]
    wq4 = w_uq.reshape(lead + (N_HEADS, QK_DIM))
    wq = _pad_last(wq4, 0, HEAD_PAD - QK_DIM).reshape(lead + (-1,)).astype(BF16)
    wqp = _pad_last(_swap_halves(wq4[..., QK_NOPE:]), QK_NOPE, HEAD_PAD - QK_DIM)
    wqp = wqp.reshape(lead + (-1,)).astype(BF16)

    wkv4 = w_ukv.reshape(lead + (N_HEADS, QK_NOPE + V_DIM))
    wknT = jnp.swapaxes(wkv4[..., :QK_NOPE].reshape(lead + (-1,)), 1, 2).astype(BF16)
    wv = wkv4[..., QK_NOPE:].reshape(lead + (-1,)).astype(BF16)

    score_bound = (QK_DIM * Q_SCALE * jnp.max(jnp.abs(q_head_g), axis=1)
                   * jnp.max(jnp.abs(k_head_g), axis=1))
    shift_lane = (np.arange(HEAD_PAD) == QK_DIM).astype(np.float32)
    qshift = -score_bound[:, None, None] * shift_lane[None, None, :]

    gq = _pad_last(q_head_g, 0, HEAD_PAD - QK_DIM)[:, None, :]
    gqp = _pad_last(_swap_halves(q_head_g[:, QK_NOPE:]), QK_NOPE, HEAD_PAD - QK_DIM)[:, None, :]
    gk = k_head_g[:, :, None]
    gkp = _swap_halves(k_head_g[:, QK_NOPE:])[:, :, None]
    proj = (norm_g[:, None, :], w_all, q_latent_g[:, None, :], kv_latent_g[:, None, :],
            wq, wqp, wknT, wv, qshift, gq, gqp, gk, gkp)
    merge = (norm_g[:, None, :], w_all, w_attn_proj.astype(BF16), w_fourier_proj.astype(BF16),
             b_merge, w_out.astype(BF16))
    return proj, merge, score_bound


def kernel(x, norm_g, w_in, q_latent_g, kv_latent_g, w_uq, w_ukv, q_head_g, k_head_g,
           w_attn_proj, w_fourier_proj, b_merge, w_out):
    fc, stage_a, stage_b = _dft_tables()
    rope = _rope_tables()
    proj_params, merge_params, score_bound = _prepare_params(
        norm_g, w_in, q_latent_g, kv_latent_g, w_uq, w_ukv, q_head_g, k_head_g,
        w_attn_proj, w_fourier_proj, b_merge, w_out)

    x2 = x.reshape(TOKENS, D_MODEL)
    for layer in range(DEPTH):
        q, kt, v, wr, wi = _proj_call(layer, x2, *proj_params, *rope, fc)
        oa = lax.cond(score_bound[layer] < MAX_SAFE_SCORE_BOUND,
                      lambda *a: _attn_call(*a, bounded=True),
                      lambda *a: _attn_call(*a, bounded=False),
                      q.reshape(BATCH, SEQ, N_HEADS * HEAD_PAD), kt,
                      v.reshape(BATCH, SEQ, ATTN_WIDTH))
        f = _dft_call(wr, wi, stage_a, stage_b)
        x2 = _merge_call(layer, x2, oa.reshape(TOKENS, ATTN_WIDTH),
                         f.reshape(TOKENS, FOURIER_WIDTH), *merge_params)
    return x2.reshape(BATCH, SEQ, D_MODEL)
```
